```python
import math
import jax, jax.numpy as jnp
from jax import lax
import numpy as np

D_MODEL = 1024
BATCH = 16
SEQ = 256
DEPTH = 2
DEC_BATCH = 8
DEC_SEQ = 1024
PAST_LEN = 512

GRID_W = 64
N_EVEN = (DEPTH + 1) // 2
N_ODD = DEPTH // 2
A_HEADS = 4
A_DK = 128
A_DV = 128
A_QK = A_HEADS * A_DK
A_WIDTH = A_HEADS * A_DV
HGRN_CHUNK = 16
B_HEADS = 4
B_DH = 64
B_DV = 2 * B_DH
B_QK = B_HEADS * 2 * B_DH
B_WIDTH = B_HEADS * B_DV
C_HEADS = 16
C_KV_HEADS = 4
C_GROUP = C_HEADS // C_KV_HEADS
C_DH = 64
WINDOW = 128
BLOCK = 128
EVEN_SPLITS = [A_QK, 2 * A_QK, 3 * A_QK, 3 * A_QK + A_WIDTH, 3 * A_QK + 2 * A_WIDTH,
               3 * A_QK + 2 * A_WIDTH + B_QK, 3 * A_QK + 2 * A_WIDTH + 2 * B_QK]
EVEN_IN = 3 * A_QK + 2 * A_WIDTH + 2 * B_QK + B_WIDTH
EVEN_MIX = A_WIDTH + B_WIDTH
ODD_SPLITS = [C_HEADS * C_DH, C_HEADS * C_DH + C_KV_HEADS * C_DH]
ODD_IN = C_HEADS * C_DH + 2 * C_KV_HEADS * C_DH
ODD_MIX = C_HEADS * C_DH
N_EXPERTS = 16
CAP_FACTOR = 2
EXPERT_FF = 1024
ALPHA = (2 * DEPTH) ** 0.25
BETA = (8 * DEPTH) ** -0.25
LN_EPS = 1e-5
ROPE_BASE = 10000.0
NEG_INF = -1e30
F32 = jnp.float32

kernel_name = 'hybrid_diffusion_hgrn2_diffattn_swa_ecmoe_step'


def to_heads(x, n_heads):
    B, L, _ = x.shape
    return x.reshape(B, L, n_heads, -1).transpose(0, 2, 1, 3)


def from_heads(x):
    B, H, L, d = x.shape
    return x.transpose(0, 2, 1, 3).reshape(B, L, H * d)


def layer_norm(x, g, b):
    xf = x.astype(F32)
    mu = jnp.mean(xf, -1, keepdims=True)
    var = jnp.mean(jnp.square(xf - mu), -1, keepdims=True)
    return ((xf - mu) * lax.rsqrt(var + LN_EPS) * g.astype(F32) + b.astype(F32)).astype(x.dtype)


def rms_norm(x, g):
    xf = x.astype(F32)
    return xf * lax.rsqrt(jnp.mean(jnp.square(xf), -1, keepdims=True) + LN_EPS) * g.astype(F32)


def rope_2d(x):
    L, d = x.shape[-2], x.shape[-1]
    half, quarter = d // 2, d // 4
    t = jnp.arange(L)
    rows = (t // GRID_W).astype(F32)
    cols = (t % GRID_W).astype(F32)
    inv = ROPE_BASE ** (-jnp.arange(quarter, dtype=F32) / quarter)

    def rot(xh, pos):
        ang = pos[:, None] * inv[None, :]
        cos, sin = jnp.cos(ang).astype(x.dtype), jnp.sin(ang).astype(x.dtype)
        x1, x2 = xh[..., :quarter], xh[..., quarter:]
        return jnp.concatenate([x1 * cos - x2 * sin, x1 * sin + x2 * cos], -1)

    return jnp.concatenate([rot(x[..., :half], rows), rot(x[..., half:], cols)], -1)


def modulation(cvec, w, b):
    m = jnp.einsum('nd,de->ne', jax.nn.silu(cvec), w) + b
    return [t[:, None, :] for t in jnp.split(m, 6, axis=-1)]


def softmax_attend(q, k, v, sink=None):
    B, Hk, G, Nq, d = q.shape
    nblk = Nq // BLOCK
    qb = jnp.moveaxis(q.reshape(B, Hk, G, nblk, BLOCK, d), 3, 0)
    scale = d ** -0.5

    def one_block(qblk):
        s = jnp.einsum('bhgqd,bhkd->bhgqk', qblk, k).astype(F32) * scale
        if sink is not None:
            col = jnp.broadcast_to(sink.astype(F32)[None, :, :, None, None], s.shape[:-1] + (1,))
            p = jax.nn.softmax(jnp.concatenate([s, col], -1), axis=-1)[..., :-1]
        else:
            p = jax.nn.softmax(s, axis=-1)
        return jnp.einsum('bhgqk,bhkd->bhgqd', p.astype(v.dtype), v)

    out = lax.map(one_block, qb)
    return jnp.moveaxis(out, 0, 3).reshape(B, Hk, G, Nq, v.shape[-1])


def window_attend(q, k, v, k_ctx, v_ctx, sink):
    B, Hk, G, L, d = q.shape
    nb = L // BLOCK
    scale = d ** -0.5

    def banded(t):
        tp = jnp.pad(t, ((0, 0), (0, 0), (BLOCK, BLOCK), (0, 0))).reshape(B, Hk, nb + 2, BLOCK, t.shape[-1])
        return jnp.concatenate([tp[:, :, :nb], tp[:, :, 1:nb + 1], tp[:, :, 2:]], axis=3)

    kw, vw = banded(k), banded(v)
    qb = q.reshape(B, Hk, G, nb, BLOCK, d)
    s_loc = jnp.einsum('bhgnqd,bhnkd->bhgnqk', qb, kw).astype(F32) * scale
    q_off = jnp.arange(BLOCK)
    k_off = jnp.arange(3 * BLOCK) - BLOCK
    k_abs = jnp.arange(nb)[:, None] * BLOCK + k_off[None, :]
    in_window = jnp.abs(k_off[None, :] - q_off[:, None]) <= WINDOW
    valid = in_window[None] & ((k_abs >= 0) & (k_abs < L))[:, None, :]
    s_loc = jnp.where(valid, s_loc, NEG_INF)
    s_ctx = jnp.einsum('bhgnqd,bhkd->bhgnqk', qb, k_ctx).astype(F32) * scale
    col = jnp.broadcast_to(sink.astype(F32)[None, :, :, None, None, None], s_loc.shape[:-1] + (1,))
    p = jax.nn.softmax(jnp.concatenate([s_loc, s_ctx, col], -1), axis=-1)
    n_loc, n_ctx = 3 * BLOCK, k_ctx.shape[2]
    out = (jnp.einsum('bhgnqk,bhnkd->bhgnqd', p[..., :n_loc].astype(v.dtype), vw)
           + jnp.einsum('bhgnqk,bhkd->bhgnqd', p[..., n_loc:n_loc + n_ctx].astype(v.dtype), v_ctx))
    return out.reshape(B, Hk, G, L, v.shape[-1])


def hgrn_scan(q, k, v, log_f, s0):
    B, H, L, dk = q.shape
    dv = v.shape[-1]
    nc = L // HGRN_CHUNK

    def chunked(t):
        return jnp.moveaxis(t.astype(F32).reshape(B, H, nc, HGRN_CHUNK, t.shape[-1]), 2, 0)

    causal = jnp.tril(jnp.ones((HGRN_CHUNK, HGRN_CHUNK), dtype=bool))

    def step(S, inp):
        qc, kc, vc, gc = inp
        b = jnp.cumsum(gc, axis=2)
        b_end = b[:, :, -1:, :]
        q_dec = qc * jnp.exp(b)
        k_dec = kc * jnp.exp(-b)
        scores = jnp.where(causal, jnp.einsum('bhcd,bhsd->bhcs', q_dec, k_dec), 0.0)
        o = jnp.einsum('bhcd,bhde->bhce', q_dec, S) + jnp.einsum('bhcs,bhse->bhce', scores, vc)
        S_new = (jnp.exp(b_end[:, :, 0, :])[..., None] * S
                 + jnp.einsum('bhsd,bhse->bhde', kc * jnp.exp(b_end - b), vc))
        return S_new, o

    S_fin, o = lax.scan(step, s0.astype(F32), (chunked(q), chunked(k), chunked(v), chunked(log_f)))
    return jnp.moveaxis(o, 0, 2).reshape(B, H, L, dv), S_fin


def even_mixer(h, lb, hgrn_g, lam, lam_init, diff_g, w_in, w_out, ctx_state=None):
    B, L, _ = h.shape
    p = jnp.einsum('bld,de->ble', h, w_in)
    q_a, f_fw, f_bw, i_a, g_a, q_b, k_b, v_b = jnp.split(p, EVEN_SPLITS, axis=-1)
    q_a, f_fw, f_bw, i_a, g_a = (to_heads(q_a, A_HEADS), to_heads(f_fw, A_HEADS), to_heads(f_bw, A_HEADS),
                                 to_heads(i_a, A_HEADS), to_heads(g_a, A_HEADS))
    q_b, k_b, v_b = to_heads(q_b, B_HEADS), to_heads(k_b, B_HEADS), to_heads(v_b, B_HEADS)
    lbr = lb.reshape(1, A_HEADS, 1, A_DK)

    def gates(z):
        z = z.astype(F32)
        return jnp.log(lbr + (1.0 - lbr) * jax.nn.sigmoid(z)), (1.0 - lbr) * jax.nn.sigmoid(-z)

    logf_fw, key_fw = gates(f_fw)
    logf_bw, key_bw = gates(f_bw)
    if ctx_state is None:
        s0_fw = jnp.zeros((B, A_HEADS, A_DK, A_DV), F32)
        s0_bw = jnp.zeros((B, A_HEADS, A_DK, A_DV), F32)
    else:
        s_hgrn, k_ctx, v_ctx = ctx_state
        s0_fw, s0_bw = s_hgrn[:, 0], s_hgrn[:, 1]
    flip = lambda t: jnp.flip(t, axis=2)
    o_fw, s_fw = hgrn_scan(q_a, key_fw, i_a, logf_fw, s0_fw)
    o_bw, s_bw = hgrn_scan(flip(q_a), flip(key_bw), flip(i_a), flip(logf_bw), s0_bw)
    o_a = rms_norm(o_fw + flip(o_bw), hgrn_g) * jax.nn.silu(g_a.astype(F32))
    if ctx_state is None:
        q1, q2 = q_b[..., :B_DH], q_b[..., B_DH:]
        k_keys, v_keys = k_b, v_b
    else:
        q1, q2 = rope_2d(q_b[..., :B_DH]), rope_2d(q_b[..., B_DH:])
        k_lat = jnp.concatenate([rope_2d(k_b[..., :B_DH]), rope_2d(k_b[..., B_DH:])], -1)
        k_keys = jnp.concatenate([k_ctx, k_lat], axis=2)
        v_keys = jnp.concatenate([v_ctx, v_b], axis=2)
    k1, k2 = k_keys[..., :B_DH], k_keys[..., B_DH:]
    lam = lam.astype(F32)
    lam_full = jnp.exp(jnp.sum(lam[0] * lam[1])) - jnp.exp(jnp.sum(lam[2] * lam[3])) + lam_init
    o1 = softmax_attend(q1[:, :, None], k1, v_keys)[:, :, 0]
    o2 = softmax_attend(q2[:, :, None], k2, v_keys)[:, :, 0]
    o_b = rms_norm(o1.astype(F32) - lam_full * o2.astype(F32), diff_g) * (1.0 - lam_init)
    mixed = jnp.concatenate([from_heads(o_a), from_heads(o_b)], -1).astype(h.dtype)
    out = jnp.einsum('ble,ed->bld', mixed, w_out)
    if ctx_state is None:
        return out, jnp.stack([s_fw, s_bw], axis=1).astype(h.dtype), k_b, v_b
    return out


def odd_mixer(h, sink, w_in, w_out, ctx_kv=None):
    B, L, _ = h.shape
    p = jnp.einsum('bld,de->ble', h, w_in)
    q, k, v = jnp.split(p, ODD_SPLITS, axis=-1)
    q = q.reshape(B, L, C_KV_HEADS, C_GROUP, C_DH).transpose(0, 2, 3, 1, 4)
    k, v = to_heads(k, C_KV_HEADS), to_heads(v, C_KV_HEADS)
    sink = sink.reshape(C_KV_HEADS, C_GROUP)
    if ctx_kv is None:
        o = softmax_attend(q, k, v, sink)
    else:
        o = window_attend(rope_2d(q), rope_2d(k), v, ctx_kv[0], ctx_kv[1], sink)
    o = o.transpose(0, 3, 1, 2, 4).reshape(B, L, ODD_MIX)
    out = jnp.einsum('ble,ed->bld', o, w_out)
    if ctx_kv is None:
        return out, k, v
    return out


def expert_choice_ffn(h, w_router, w_gate, w_up, w_down):
    B, N, D = h.shape
    cap = (CAP_FACTOR * N) // N_EXPERTS
    aff = jax.nn.softmax(jnp.einsum('bnd,de->bne', h, w_router).astype(F32), axis=-1)
    gates, idx = lax.top_k(jnp.swapaxes(aff, 1, 2), cap)
    bidx = jnp.arange(B)[:, None, None]
    xs = h[bidx, idx]
    hid = jax.nn.silu(jnp.einsum('becd,edf->becf', xs, w_gate)) * jnp.einsum('becd,edf->becf', xs, w_up)
    out = (jnp.einsum('becf,efd->becd', hid, w_down) * gates[..., None]).astype(h.dtype)
    return jnp.zeros_like(h).at[bidx, idx].add(out)


def setup_inputs(seed: int = 0) -> dict:
    key = jax.random.key(seed)
    ks = jax.random.split(key, 32)
    D = D_MODEL

    def nrm(k, shape, s):
        return jax.random.normal(k, shape, F32) * s

    return {
        'x_prompt': nrm(ks[0], (BATCH, SEQ, D), 1.0),
        'x_sample': nrm(ks[1], (DEC_BATCH, DEC_SEQ, D), 1.0),
        'state_hgrn': nrm(ks[2], (DEC_BATCH, N_EVEN, 2, A_HEADS, A_DK, A_DV), 0.5),
        'cache_diff_k': nrm(ks[3], (DEC_BATCH, N_EVEN, B_HEADS, PAST_LEN, 2 * B_DH), 1.0),
        'cache_diff_v': nrm(ks[4], (DEC_BATCH, N_EVEN, B_HEADS, PAST_LEN, B_DV), 1.0),
        'cache_win_k': nrm(ks[5], (DEC_BATCH, N_ODD, C_KV_HEADS, PAST_LEN, C_DH), 1.0),
        'cache_win_v': nrm(ks[6], (DEC_BATCH, N_ODD, C_KV_HEADS, PAST_LEN, C_DH), 1.0),
        'c': nrm(ks[7], (DEC_BATCH, D), 1.0),
        'c_ctx': nrm(ks[8], (D,), 1.0),
        'w_mod': nrm(ks[9], (DEPTH, D, 6 * D), D ** -0.5),
        'b_mod': nrm(ks[10], (DEPTH, 6 * D), 0.01),
        'ln_g': 1.0 + nrm(ks[11], (DEPTH, 2, D), 0.02),
        'ln_b': nrm(ks[12], (DEPTH, 2, D), 0.02),
        'w_in_even': nrm(ks[13], (N_EVEN, D, EVEN_IN), D ** -0.5),
        'w_out_even': nrm(ks[14], (N_EVEN, EVEN_MIX, D), EVEN_MIX ** -0.5 * BETA),
        'hgrn_lb_logits': nrm(ks[15], (DEPTH + 1, A_QK), 0.5),
        'hgrn_norm_g': 1.0 + nrm(ks[16], (N_EVEN, A_DV), 0.02),
        'diff_lambda': nrm(ks[17], (N_EVEN, 4, B_DH), 0.1),
        'diff_norm_g': 1.0 + nrm(ks[18], (N_EVEN, B_DV), 0.02),
        'w_in_odd': nrm(ks[19], (N_ODD, D, ODD_IN), D ** -0.5),
        'w_out_odd': nrm(ks[20], (N_ODD, ODD_MIX, D), ODD_MIX ** -0.5 * BETA),
        'win_sink': nrm(ks[21], (N_ODD, C_HEADS), 0.5),
        'w_router': nrm(ks[22], (DEPTH, D, N_EXPERTS), D ** -0.5),
        'w_exp_gate': nrm(ks[23], (DEPTH, N_EXPERTS, D, EXPERT_FF), D ** -0.5),
        'w_exp_up': nrm(ks[24], (DEPTH, N_EXPERTS, D, EXPERT_FF), D ** -0.5),
        'w_exp_down': nrm(ks[25], (DEPTH, N_EXPERTS, EXPERT_FF, D), EXPERT_FF ** -0.5 * BETA),
    }


def reference(x_prompt, x_sample, state_hgrn, cache_diff_k, cache_diff_v, cache_win_k, cache_win_v,
              c, c_ctx, w_mod, b_mod, ln_g, ln_b, w_in_even, w_out_even, hgrn_lb_logits, hgrn_norm_g,
              diff_lambda, diff_norm_g, w_in_odd, w_out_odd, win_sink, w_router, w_exp_gate, w_exp_up,
              w_exp_down):
    yp, ys = x_prompt, x_sample
    lb_all = jnp.cumsum(jax.nn.softmax(hgrn_lb_logits.astype(F32), axis=0), axis=0)
    st_h, st_dk, st_dv, st_wk, st_wv = [], [], [], [], []
    for l in range(DEPTH):
        mp = modulation(c_ctx[None, :], w_mod[l], b_mod[l])
        ms = modulation(c, w_mod[l], b_mod[l])
        hp = yp * (1 + mp[1]) + mp[0]
        hs = ys * (1 + ms[1]) + ms[0]
        j = l // 2
        if l % 2 == 0:
            lam_init = 0.8 - 0.6 * math.exp(-0.3 * l)
            args = (lb_all[l], hgrn_norm_g[j], diff_lambda[j], lam_init, diff_norm_g[j], w_in_even[j], w_out_even[j])
            out_p, s_ctx, k_ctx, v_ctx = even_mixer(hp, *args)
            out_s = even_mixer(hs, *args, ctx_state=(state_hgrn[:, j], cache_diff_k[:, j], cache_diff_v[:, j]))
            st_h.append(s_ctx)
            st_dk.append(k_ctx)
            st_dv.append(v_ctx)
        else:
            out_p, k_ctx, v_ctx = odd_mixer(hp, win_sink[j], w_in_odd[j], w_out_odd[j])
            out_s = odd_mixer(hs, win_sink[j], w_in_odd[j], w_out_odd[j], ctx_kv=(cache_win_k[:, j], cache_win_v[:, j]))
            st_wk.append(k_ctx)
            st_wv.append(v_ctx)
        yp = layer_norm(ALPHA * yp + mp[2] * out_p, ln_g[l, 0], ln_b[l, 0])
        ys = layer_norm(ALPHA * ys + ms[2] * out_s, ln_g[l, 0], ln_b[l, 0])
        hp = yp * (1 + mp[4]) + mp[3]
        hs = ys * (1 + ms[4]) + ms[3]
        ffp = expert_choice_ffn(hp, w_router[l], w_exp_gate[l], w_exp_up[l], w_exp_down[l])
        ffs = expert_choice_ffn(hs, w_router[l], w_exp_gate[l], w_exp_up[l], w_exp_down[l])
        yp = layer_norm(ALPHA * yp + mp[5] * ffp, ln_g[l, 1], ln_b[l, 1])
        ys = layer_norm(ALPHA * ys + ms[5] * ffs, ln_g[l, 1], ln_b[l, 1])
    new_state_hgrn = jnp.stack(st_h, axis=1)
    new_cache_diff_k = jnp.stack(st_dk, axis=1)
    new_cache_diff_v = jnp.stack(st_dv, axis=1)
    new_cache_win_k = jnp.stack(st_wk, axis=1)
    new_cache_win_v = jnp.stack(st_wv, axis=1)
    return (yp, ys, new_state_hgrn, new_cache_diff_k, new_cache_diff_v, new_cache_win_k, new_cache_win_v)
```

```python
import functools
import math

import jax
import jax.numpy as jnp
from jax import lax
from jax.experimental import pallas as pl
from jax.experimental.pallas import tpu as pltpu

F32 = jnp.float32
BF16 = jnp.bfloat16
I32 = jnp.int32

D_MODEL = 1024
DEPTH = 2
GRID_W = 64
A_HEADS = 4
A_DK = 128
A_DV = 128
HGRN_CHUNK = 16
B_HEADS = 4
B_DH = 64
C_HEADS = 16
C_KV_HEADS = 4
C_GROUP = C_HEADS // C_KV_HEADS
C_DH = 64
WINDOW = 128
N_EXPERTS = 16
CAP_FACTOR = 2
ALPHA = (2 * DEPTH) ** 0.25
LN_EPS = 1e-5
ROPE_BASE = 10000.0
NEG_INF = -1e30
LANES = 128
VMEM_LIMIT = 56 * 1024 * 1024


def _cparams(n_axes):
    return pltpu.CompilerParams(dimension_semantics=("arbitrary",) * n_axes, vmem_limit_bytes=VMEM_LIMIT)


def _dot(a, b):
    return jnp.dot(a.astype(BF16), b.astype(BF16), preferred_element_type=F32)


def _dot_nt(a, b):
    return lax.dot_general(a.astype(BF16), b.astype(BF16), (((1,), (1,)), ((), ())), preferred_element_type=F32)


def _dot_tn(a, b):
    return lax.dot_general(a.astype(BF16), b.astype(BF16), (((0,), (0,)), ((), ())), preferred_element_type=F32)


def _split3(x):
    x1 = x.astype(BF16)
    r1 = x - x1.astype(F32)
    x2 = r1.astype(BF16)
    x3 = (r1 - x2.astype(F32)).astype(BF16)
    return x1, x2, x3


def _dot_sel(m, x):
    x1, x2, x3 = _split3(x)
    return (jnp.dot(m, x1, preferred_element_type=F32) + jnp.dot(m, x2, preferred_element_type=F32)
            + jnp.dot(m, x3, preferred_element_type=F32))


def _mod_kernel(c_ref, w_ref, b_ref, o_ref):
    a = jax.nn.silu(c_ref[...])
    o_ref[0] = jnp.dot(a, w_ref[0], precision=lax.Precision.HIGHEST, preferred_element_type=F32) + b_ref[0]


def _modulation(cvec, w_mod, b_mod):
    n, d = cvec.shape
    tn = 1024
    nt = w_mod.shape[2] // tn
    return pl.pallas_call(
        _mod_kernel,
        grid=(DEPTH, nt),
        in_specs=[pl.BlockSpec((n, d), lambda l, j: (0, 0)),
                  pl.BlockSpec((1, d, tn), lambda l, j: (l, 0, j)),
                  pl.BlockSpec((1, 1, tn), lambda l, j: (l, 0, j))],
        out_specs=pl.BlockSpec((1, n, tn), lambda l, j: (l, 0, j)),
        out_shape=jax.ShapeDtypeStruct((DEPTH, n, w_mod.shape[2]), F32),
        compiler_params=_cparams(2),
        name="modulation",
    )(cvec, w_mod, b_mod.reshape(DEPTH, 1, -1))


def _mod_spec(per_batch, n_axes):
    if n_axes == 1:
        return pl.BlockSpec((1, 8, D_MODEL), (lambda b: (b, 0, 0)) if per_batch else (lambda b: (0, 0, 0)))
    return pl.BlockSpec((1, 8, D_MODEL), (lambda b, i: (b, 0, 0)) if per_batch else (lambda b, i: (0, 0, 0)))


def _inproj_kernel(x_ref, mod_ref, w_ref, o_ref):
    h = x_ref[0] * (1.0 + mod_ref[0, 1:2, :]) + mod_ref[0, 0:1, :]
    o_ref[0] = jnp.dot(h.astype(BF16), w_ref[...], preferred_element_type=F32)


def _inproj(x, mod, w_bf16, per_batch):
    B, L, D = x.shape
    N = w_bf16.shape[1]
    tm = 256
    return pl.pallas_call(
        _inproj_kernel,
        grid=(B, L // tm),
        in_specs=[pl.BlockSpec((1, tm, D), lambda b, i: (b, i, 0)),
                  _mod_spec(per_batch, 2),
                  pl.BlockSpec((D, N), lambda b, i: (0, 0))],
        out_specs=pl.BlockSpec((1, tm, N), lambda b, i: (b, i, 0)),
        out_shape=jax.ShapeDtypeStruct((B, L, N), F32),
        compiler_params=_cparams(2),
        name="inproj",
    )(x, mod, w_bf16)


def _hgrn_kernel(q_ref, zf_ref, zb_ref, v_ref, g_ref, lb_ref, gn_ref, s0_ref, o_ref, sfin_ref, *, nblk):
    blk = LANES
    nch = blk // HGRN_CHUNK
    lb = lb_ref[0]
    gn = gn_ref[...]
    r = lax.broadcasted_iota(I32, (blk, blk), 0)
    c = lax.broadcasted_iota(I32, (blk, blk), 1)
    same = (r // HGRN_CHUNK) == (c // HGRN_CHUNK)
    ones_blk = jnp.where(same, 1.0, 0.0).astype(BF16)
    rowchunk = r // HGRN_CHUNK
    for d in (0, 1):
        tri = jnp.logical_and(same, (c <= r) if d == 0 else (c >= r))
        tri_b = jnp.where(tri, 1.0, 0.0).astype(BF16)
        z_ref = zf_ref if d == 0 else zb_ref

        def blk_body(i, st, d=d, tri=tri, tri_b=tri_b, z_ref=z_ref):
            bi = i if d == 0 else nblk - 1 - i
            rows = pl.ds(pl.multiple_of(bi * blk, blk), blk)
            z = z_ref[0, rows, :]
            q = q_ref[0, rows, :]
            v = v_ref[0, rows, :]
            logf = jnp.log(lb + (1.0 - lb) * jax.nn.sigmoid(z))
            key = (1.0 - lb) * jax.nn.sigmoid(-z)
            g1, g2, g3 = _split3(logf)
            b = (jnp.dot(tri_b, g1, preferred_element_type=F32) + jnp.dot(tri_b, g2, preferred_element_type=F32)
                 + jnp.dot(tri_b, g3, preferred_element_type=F32))
            tot = (jnp.dot(ones_blk, g1, preferred_element_type=F32) + jnp.dot(ones_blk, g2, preferred_element_type=F32)
                   + jnp.dot(ones_blk, g3, preferred_element_type=F32))
            qd = (q * jnp.exp(b)).astype(BF16)
            kd = (key * jnp.exp(-b)).astype(BF16)
            ke = key * jnp.exp(tot - b)
            vb = v.astype(BF16)
            scores = jnp.where(tri, _dot_nt(qd, kd), 0.0)
            o = _dot(scores, vb)
            ke_exp = jnp.concatenate([jnp.where(rowchunk == cc, ke, 0.0).astype(BF16) for cc in range(nch)], axis=1)
            ut = jnp.dot(v.T.astype(BF16), ke_exp, preferred_element_type=F32)
            dec = jnp.exp(tot)
            outs = [None] * nch
            for cc in (range(nch) if d == 0 else reversed(range(nch))):
                lo = cc * HGRN_CHUNK
                outs[cc] = _dot_nt(qd[lo:lo + HGRN_CHUNK], st)
                st = st * dec[lo:lo + 1, :] + ut[:, cc * A_DK:(cc + 1) * A_DK]
            o = o + jnp.concatenate(outs, axis=0)
            if d == 0:
                o_ref[0, rows, :] = o
            else:
                t = o_ref[0, rows, :] + o
                ms = jnp.mean(jnp.square(t), axis=-1, keepdims=True)
                o_ref[0, rows, :] = t * lax.rsqrt(ms + LN_EPS) * gn * jax.nn.silu(g_ref[0, rows, :])
            return st

        st = lax.fori_loop(0, nblk, blk_body, s0_ref[0, d, 0].T)
        sfin_ref[0, d, 0] = st.T


def _hgrn(p, lb, gn, s0):
    B, L, _ = p.shape
    H = A_HEADS

    def col(off):
        return pl.BlockSpec((1, L, LANES), lambda b, h, off=off: (b, 0, off + h))

    st_spec = pl.BlockSpec((1, 2, 1, A_DK, A_DV), lambda b, h: (b, 0, h, 0, 0))
    return pl.pallas_call(
        functools.partial(_hgrn_kernel, nblk=L // LANES),
        grid=(B, H),
        in_specs=[col(0), col(H), col(2 * H), col(3 * H), col(4 * H),
                  pl.BlockSpec((1, 1, A_DK), lambda b, h: (h, 0, 0)),
                  pl.BlockSpec((1, A_DV), lambda b, h: (0, 0)),
                  st_spec],
        out_specs=[pl.BlockSpec((1, L, LANES), lambda b, h: (b, 0, h)), st_spec],
        out_shape=[jax.ShapeDtypeStruct((B, L, H * A_DV), F32), jax.ShapeDtypeStruct((B, 2, H, A_DK, A_DV), F32)],
        compiler_params=_cparams(2),
        name="hgrn",
    )(p, p, p, p, p, lb.reshape(H, 1, A_DK), gn.reshape(1, A_DV), s0)


def _rope_tables(L, width):
    quarter = C_DH // 4
    t = jnp.arange(L)
    rows = (t // GRID_W).astype(F32)
    cols = (t % GRID_W).astype(F32)
    inv = ROPE_BASE ** (-jnp.arange(quarter, dtype=F32) / quarter)
    ang_r = rows[:, None] * inv[None, :]
    ang_c = cols[:, None] * inv[None, :]
    cos = jnp.concatenate([jnp.cos(ang_r), jnp.cos(ang_r), jnp.cos(ang_c), jnp.cos(ang_c)], -1)
    sin = jnp.concatenate([-jnp.sin(ang_r), jnp.sin(ang_r), -jnp.sin(ang_c), jnp.sin(ang_c)], -1)
    reps = width // C_DH
    return jnp.tile(cos, (1, reps)), jnp.tile(sin, (1, reps))


def _rope(x, cos, sin):
    n = x.shape[-1]
    q = C_DH // 4
    lane = lax.broadcasted_iota(I32, x.shape, x.ndim - 1)
    partner = jnp.where((lane & q) == 0, pltpu.roll(x, n - q, axis=x.ndim - 1), pltpu.roll(x, q, axis=x.ndim - 1))
    return x * cos + partner * sin


def _diff_kernel(*refs, L, n_ctx, lam_init, tq):
    if n_ctx:
        q_ref, k_ref, v_ref, lam_ref, gn_ref, kc_ref, vc_ref, cos_ref, sin_ref, o_ref, kk_scr, vv_scr = refs
    else:
        q_ref, k_ref, v_ref, lam_ref, gn_ref, o_ref, kk_scr, vv_scr = refs
    lam = lam_ref[...]
    lam_full = (jnp.exp(jnp.sum(lam[0:1] * lam[1:2], keepdims=True)) - jnp.exp(jnp.sum(lam[2:3] * lam[3:4], keepdims=True))
                + lam_init)
    gn = gn_ref[...]
    if n_ctx:
        kk_scr[0:n_ctx, :] = kc_ref[0, 0, 0].astype(BF16)
        vv_scr[0:n_ctx, :] = vc_ref[0, 0, 0].astype(BF16)
        kk_scr[n_ctx:, :] = _rope(k_ref[0], cos_ref[...], sin_ref[...]).astype(BF16)
    else:
        kk_scr[...] = k_ref[0].astype(BF16)
    vv_scr[n_ctx:, :] = v_ref[0].astype(BF16)
    scale = B_DH ** -0.5

    def body(i, carry):
        rows = pl.ds(pl.multiple_of(i * tq, tq), tq)
        q = q_ref[0, rows, :]
        if n_ctx:
            q = _rope(q, cos_ref[rows, :], sin_ref[rows, :])
        qb = (q * scale).astype(BF16)
        kk = kk_scr[...]
        vv = vv_scr[...]
        outs = []
        for part in (0, 1):
            sl = slice(part * B_DH, (part + 1) * B_DH)
            s = lax.dot_general(qb[:, sl], kk[:, sl], (((1,), (1,)), ((), ())), preferred_element_type=F32)
            m = jnp.max(s, axis=-1, keepdims=True)
            e = jnp.exp(s - m)
            den = jnp.sum(e, axis=-1, keepdims=True)
            outs.append(jnp.dot(e.astype(BF16), vv, preferred_element_type=F32) / den)
        o = outs[0] - lam_full * outs[1]
        ms = jnp.mean(jnp.square(o), axis=-1, keepdims=True)
        o_ref[0, rows, :] = o * lax.rsqrt(ms + LN_EPS) * gn * (1.0 - lam_init)
        return carry

    lax.fori_loop(0, L // tq, body, 0)


def _diff_attn(p, lam, gn, lam_init, ctx=None):
    B, L, _ = p.shape
    H = B_HEADS
    off = 5 * A_HEADS

    def col(o):
        return pl.BlockSpec((1, L, LANES), lambda b, h, o=o: (b, 0, o + h))

    in_specs = [col(off), col(off + H), col(off + 2 * H),
                pl.BlockSpec(lam.shape, lambda b, h: (0, 0)),
                pl.BlockSpec((1, LANES), lambda b, h: (0, 0))]
    args = [p, p, p, lam, gn.reshape(1, LANES)]
    n_ctx = 0
    if ctx is not None:
        kc, vc, j = ctx
        n_ctx = kc.shape[3]
        cos, sin = _rope_tables(L, LANES)
        in_specs += [pl.BlockSpec((1, 1, 1, n_ctx, LANES), lambda b, h, j=j: (b, j, h, 0, 0)),
                     pl.BlockSpec((1, 1, 1, n_ctx, LANES), lambda b, h, j=j: (b, j, h, 0, 0)),
                     pl.BlockSpec((L, LANES), lambda b, h: (0, 0)),
                     pl.BlockSpec((L, LANES), lambda b, h: (0, 0))]
        args += [kc, vc, cos, sin]
    return pl.pallas_call(
        functools.partial(_diff_kernel, L=L, n_ctx=n_ctx, lam_init=lam_init, tq=min(L, 256)),
        grid=(B, H),
        in_specs=in_specs,
        out_specs=pl.BlockSpec((1, L, LANES), lambda b, h: (b, 0, h)),
        out_shape=jax.ShapeDtypeStruct((B, L, H * LANES), F32),
        scratch_shapes=[pltpu.VMEM((n_ctx + L, LANES), BF16), pltpu.VMEM((n_ctx + L, LANES), BF16)],
        compiler_params=_cparams(2),
        name="diff_attn",
    )(*args)


def _sink_col(sink_ref, hk, rows):
    return jnp.concatenate([jnp.full((rows, 1), sink_ref[hk * C_GROUP + g], F32) for g in range(C_GROUP)], axis=0)


def _gqa_full_kernel(sink_ref, q_ref, k_ref, v_ref, o_ref, *, L):
    hp = pl.program_id(1)
    scale = C_DH ** -0.5
    q = q_ref[0] * scale
    kb = k_ref[0].astype(BF16)
    vb = v_ref[0].astype(BF16)
    for kh in (0, 1):
        hk = hp * 2 + kh
        ksl = slice(kh * C_DH, (kh + 1) * C_DH)
        q4 = jnp.concatenate([q[:, (kh * C_GROUP + g) * C_DH:(kh * C_GROUP + g + 1) * C_DH] for g in range(C_GROUP)], axis=0)
        s = lax.dot_general(q4.astype(BF16), kb[:, ksl], (((1,), (1,)), ((), ())), preferred_element_type=F32)
        sink = _sink_col(sink_ref, hk, L)
        m = jnp.maximum(jnp.max(s, axis=-1, keepdims=True), sink)
        e = jnp.exp(s - m)
        den = jnp.sum(e, axis=-1, keepdims=True) + jnp.exp(sink - m)
        o4 = jnp.dot(e.astype(BF16), vb[:, ksl], preferred_element_type=F32) / den
        for g in range(C_GROUP):
            c0 = (kh * C_GROUP + g) * C_DH
            o_ref[0, :, c0:c0 + C_DH] = o4[g * L:(g + 1) * L]


def _gqa_win_kernel(sink_ref, q_ref, k_ref, v_ref, kc_ref, vc_ref, cosq_ref, sinq_ref, o_ref, kr_scr, *, L):
    hp = pl.program_id(1)
    tq = WINDOW
    nloc = 3 * tq
    scale = C_DH ** -0.5
    kr_scr[...] = _rope(k_ref[0], cosq_ref[:, 0:LANES], sinq_ref[:, 0:LANES]).astype(BF16)

    def body(n, carry):
        r0 = pl.multiple_of(n * tq, tq)
        start = pl.multiple_of(jnp.clip((n - 1) * tq, 0, L - nloc), tq)
        rows = pl.ds(r0, tq)
        q = _rope(q_ref[0, rows, :], cosq_ref[rows, :], sinq_ref[rows, :]) * scale
        kl = kr_scr[pl.ds(start, nloc), :]
        vl = v_ref[0, pl.ds(start, nloc), :].astype(BF16)
        qpos = r0 + (lax.broadcasted_iota(I32, (C_GROUP * tq, nloc), 0) % tq)
        kpos = start + lax.broadcasted_iota(I32, (C_GROUP * tq, nloc), 1)
        valid = jnp.abs(kpos - qpos) <= WINDOW
        for kh in (0, 1):
            hk = hp * 2 + kh
            ksl = slice(kh * C_DH, (kh + 1) * C_DH)
            q4 = jnp.concatenate([q[:, (kh * C_GROUP + g) * C_DH:(kh * C_GROUP + g + 1) * C_DH] for g in range(C_GROUP)],
                                 axis=0).astype(BF16)
            s_loc = lax.dot_general(q4, kl[:, ksl], (((1,), (1,)), ((), ())), preferred_element_type=F32)
            s_loc = jnp.where(valid, s_loc, NEG_INF)
            s_ctx = lax.dot_general(q4, kc_ref[0, 0, kh].astype(BF16), (((1,), (1,)), ((), ())), preferred_element_type=F32)
            sink = _sink_col(sink_ref, hk, tq)
            m = jnp.maximum(jnp.maximum(jnp.max(s_loc, axis=-1, keepdims=True), jnp.max(s_ctx, axis=-1, keepdims=True)), sink)
            e_loc = jnp.exp(s_loc - m)
            e_ctx = jnp.exp(s_ctx - m)
            den = jnp.sum(e_loc, axis=-1, keepdims=True) + jnp.sum(e_ctx, axis=-1, keepdims=True) + jnp.exp(sink - m)
            o4 = (jnp.dot(e_loc.astype(BF16), vl[:, ksl], preferred_element_type=F32)
                  + jnp.dot(e_ctx.astype(BF16), vc_ref[0, 0, kh].astype(BF16), preferred_element_type=F32)) / den
            for g in range(C_GROUP):
                c0 = (kh * C_GROUP + g) * C_DH
                o_ref[0, rows, c0:c0 + C_DH] = o4[g * tq:(g + 1) * tq]
        return carry

    lax.fori_loop(0, L // tq, body, 0)


def _gqa(p, sink, ctx=None):
    B, L, _ = p.shape
    qw = 2 * C_GROUP * C_DH
    nq = C_HEADS * C_DH // LANES
    nk = C_KV_HEADS * C_DH // LANES
    in_specs = [pl.BlockSpec(memory_space=pltpu.SMEM),
                pl.BlockSpec((1, L, qw), lambda b, h: (b, 0, h)),
                pl.BlockSpec((1, L, LANES), lambda b, h: (b, 0, nq + h)),
                pl.BlockSpec((1, L, LANES), lambda b, h: (b, 0, nq + nk + h))]
    args = [sink, p, p, p]
    scratch = []
    if ctx is None:
        body = functools.partial(_gqa_full_kernel, L=L)
    else:
        kc, vc, j = ctx
        n_ctx = kc.shape[3]
        cos, sin = _rope_tables(L, qw)
        in_specs += [pl.BlockSpec((1, 1, 2, n_ctx, C_DH), lambda b, h, j=j: (b, j, h, 0, 0)),
                     pl.BlockSpec((1, 1, 2, n_ctx, C_DH), lambda b, h, j=j: (b, j, h, 0, 0)),
                     pl.BlockSpec((L, qw), lambda b, h: (0, 0)),
                     pl.BlockSpec((L, qw), lambda b, h: (0, 0))]
        args += [kc, vc, cos, sin]
        scratch = [pltpu.VMEM((L, LANES), BF16)]
        body = functools.partial(_gqa_win_kernel, L=L)
    return pl.pallas_call(
        body,
        grid=(B, C_KV_HEADS // 2),
        in_specs=in_specs,
        out_specs=pl.BlockSpec((1, L, qw), lambda b, h: (b, 0, h)),
        out_shape=jax.ShapeDtypeStruct((B, L, C_HEADS * C_DH), F32),
        scratch_shapes=scratch,
        compiler_params=_cparams(2),
        name="gqa",
    )(*args)


def _layer_norm(t, g, b):
    mu = jnp.mean(t, axis=-1, keepdims=True)
    tc = t - mu
    var = jnp.mean(jnp.square(tc), axis=-1, keepdims=True)
    return tc * lax.rsqrt(var + LN_EPS) * g + b


def _outproj_kernel(*refs, n_in):
    y_ref = refs[0]
    a_refs = refs[1:1 + n_in]
    mod_ref, w_ref, lng_ref, lnb_ref, wr_ref, yo_ref, h2_ref, lg_ref = refs[1 + n_in:]
    mix = None
    k0 = 0
    for a_ref in a_refs:
        kw = a_ref.shape[2]
        part = jnp.dot(a_ref[0].astype(BF16), w_ref[k0:k0 + kw, :], preferred_element_type=F32)
        mix = part if mix is None else mix + part
        k0 += kw
    yn = _layer_norm(ALPHA * y_ref[0] + mod_ref[0, 2:3, :] * mix, lng_ref[...], lnb_ref[...])
    yo_ref[0] = yn
    h2 = yn * (1.0 + mod_ref[0, 4:5, :]) + mod_ref[0, 3:4, :]
    h2_ref[0] = h2.astype(BF16)
    lg_ref[0] = jnp.dot(h2, wr_ref[...], precision=lax.Precision.HIGHEST, preferred_element_type=F32)


def _outproj(y, mixes, mod, w_bf16, lng, lnb, wr_pad, per_batch):
    B, L, D = y.shape
    tm = 256
    n_in = len(mixes)
    in_specs = [pl.BlockSpec((1, tm, D), lambda b, i: (b, i, 0))]
    in_specs += [pl.BlockSpec((1, tm, m.shape[2]), lambda b, i: (b, i, 0)) for m in mixes]
    in_specs += [_mod_spec(per_batch, 2),
                 pl.BlockSpec(w_bf16.shape, lambda b, i: (0, 0)),
                 pl.BlockSpec((1, D), lambda b, i: (0, 0)),
                 pl.BlockSpec((1, D), lambda b, i: (0, 0)),
                 pl.BlockSpec((D, LANES), lambda b, i: (0, 0))]
    return pl.pallas_call(
        functools.partial(_outproj_kernel, n_in=n_in),
        grid=(B, L // tm),
        in_specs=in_specs,
        out_specs=[pl.BlockSpec((1, tm, D), lambda b, i: (b, i, 0)),
                   pl.BlockSpec((1, tm, D), lambda b, i: (b, i, 0)),
                   pl.BlockSpec((1, tm, LANES), lambda b, i: (b, i, 0))],
        out_shape=[jax.ShapeDtypeStruct((B, L, D), F32), jax.ShapeDtypeStruct((B, L, D), BF16),
                   jax.ShapeDtypeStruct((B, L, LANES), F32)],
        compiler_params=_cparams(2),
        name="outproj",
    )(y, *mixes, mod, w_bf16, lng.reshape(1, D), lnb.reshape(1, D), wr_pad)


def _route_kernel(lg_ref, pos_ref, gs_ref, *, N, cap):
    E = N_EXPERTS
    lg = lg_ref[0]
    lane = lax.broadcasted_iota(I32, lg.shape, 1)
    lgm = jnp.where(lane < E, lg, -jnp.inf)
    m = jnp.max(lgm, axis=-1, keepdims=True)
    ex = jnp.exp(lgm - m)
    aff = ex / jnp.sum(ex, axis=-1, keepdims=True)
    aff_t = aff.T[0:E, :]
    bits = pltpu.bitcast(aff_t, I32)

    def bisect(_, carry):
        lo, hi = carry
        mid = lo + jnp.right_shift(hi - lo + 1, 1)
        cnt = jnp.sum((bits >= mid).astype(I32), axis=-1, keepdims=True)
        ok = cnt >= cap
        return jnp.where(ok, mid, lo), jnp.where(ok, hi, mid - 1)

    lo0 = jnp.zeros((E, 1), I32)
    hi0 = jnp.full((E, 1), 0x7F800000, I32)
    thr, _ = lax.fori_loop(0, 31, bisect, (lo0, hi0))
    gt = bits > thr
    eq = bits == thr
    need = cap - jnp.sum(gt.astype(I32), axis=-1, keepdims=True)
    upper = jnp.where(lax.broadcasted_iota(I32, (N, N), 0) < lax.broadcasted_iota(I32, (N, N), 1), 1.0, 0.0).astype(BF16)
    tie_rank = jnp.dot(jnp.where(eq, 1.0, 0.0).astype(BF16), upper, preferred_element_type=F32)
    sel = jnp.logical_or(gt, jnp.logical_and(eq, tie_rank < need.astype(F32)))
    slot = jnp.dot(jnp.where(sel, 1.0, 0.0).astype(BF16), upper, preferred_element_type=F32)
    pos = jnp.where(sel, slot.astype(I32), -1)
    pos_ref[0] = pos
    slot_id = lax.broadcasted_iota(I32, (cap, N), 0)
    for e in range(E):
        hit = slot_id == pos[e:e + 1, :]
        gcol = jnp.sum(jnp.where(hit, aff_t[e:e + 1, :], 0.0), axis=-1, keepdims=True)
        gs_ref[0, e] = jnp.broadcast_to(gcol, (cap, LANES))


def _route(logits):
    B, N, _ = logits.shape
    cap = (CAP_FACTOR * N) // N_EXPERTS
    return pl.pallas_call(
        functools.partial(_route_kernel, N=N, cap=cap),
        grid=(B,),
        in_specs=[pl.BlockSpec((1, N, LANES), lambda b: (b, 0, 0))],
        out_specs=[pl.BlockSpec((1, N_EXPERTS, N), lambda b: (b, 0, 0)),
                   pl.BlockSpec((1, N_EXPERTS, cap, LANES), lambda b: (b, 0, 0, 0))],
        out_shape=[jax.ShapeDtypeStruct((B, N_EXPERTS, N), I32),
                   jax.ShapeDtypeStruct((B, N_EXPERTS, cap, LANES), F32)],
        compiler_params=_cparams(1),
        name="route",
    )(logits)


def _gather_kernel(pos_ref, h_ref, xs_ref, *, N, cap):
    h = h_ref[0]
    slot_id = lax.broadcasted_iota(I32, (cap, N), 0)
    for e in range(N_EXPERTS):
        onehot = jnp.where(slot_id == pos_ref[0, e:e + 1, :], 1.0, 0.0).astype(BF16)
        xs_ref[e] = jnp.dot(onehot, h, preferred_element_type=F32).astype(BF16)


def _gather(pos, h2):
    B, N, D = h2.shape
    cap = (CAP_FACTOR * N) // N_EXPERTS
    return pl.pallas_call(
        functools.partial(_gather_kernel, N=N, cap=cap),
        grid=(B,),
        in_specs=[pl.BlockSpec((1, N_EXPERTS, N), lambda b: (b, 0, 0)),
                  pl.BlockSpec((1, N, D), lambda b: (b, 0, 0))],
        out_specs=pl.BlockSpec((N_EXPERTS, cap, D), lambda b: (0, b, 0)),
        out_shape=jax.ShapeDtypeStruct((N_EXPERTS, B * cap, D), BF16),
        compiler_params=_cparams(1),
        name="gather",
    )(pos, h2)


def _ffn_kernel(xp_ref, xs_ref, gp_ref, gs_ref, wg_ref, wu_ref, wd_ref, op_ref, os_ref, *, tr):
    wg = wg_ref[0, 0].astype(BF16)
    wu = wu_ref[0, 0].astype(BF16)
    wd = wd_ref[0, 0].astype(BF16)
    for x_ref, g_ref, o_ref in ((xp_ref, gp_ref, op_ref), (xs_ref, gs_ref, os_ref)):
        rows = x_ref.shape[1]
        cap = g_ref.shape[2]
        for r0 in range(0, rows, tr):
            x = x_ref[0, r0:r0 + tr, :]
            hid = jax.nn.silu(jnp.dot(x, wg, preferred_element_type=F32)) * jnp.dot(x, wu, preferred_element_type=F32)
            out = jnp.dot(hid.astype(BF16), wd, preferred_element_type=F32)
            gate = g_ref[r0 // cap:(r0 + tr) // cap, 0].reshape(tr, LANES)[:, 0:1]
            o_ref[0, r0:r0 + tr, :] = (out * gate).astype(BF16)


def _ffn(xs_p, xs_s, gs_p, gs_s, wg, wu, wd, l):
    E, rp, D = xs_p.shape
    rs = xs_s.shape[1]
    FF = wg.shape[3]
    tr = 512

    def gspec(g):
        return pl.BlockSpec((g.shape[0], 1, g.shape[2], LANES), lambda e: (0, e, 0, 0))

    return pl.pallas_call(
        functools.partial(_ffn_kernel, tr=tr),
        grid=(E,),
        in_specs=[pl.BlockSpec((1, rp, D), lambda e: (e, 0, 0)),
                  pl.BlockSpec((1, rs, D), lambda e: (e, 0, 0)),
                  gspec(gs_p), gspec(gs_s),
                  pl.BlockSpec((1, 1, D, FF), lambda e, l=l: (l, e, 0, 0)),
                  pl.BlockSpec((1, 1, D, FF), lambda e, l=l: (l, e, 0, 0)),
                  pl.BlockSpec((1, 1, FF, D), lambda e, l=l: (l, e, 0, 0))],
        out_specs=[pl.BlockSpec((1, rp, D), lambda e: (e, 0, 0)),
                   pl.BlockSpec((1, rs, D), lambda e: (e, 0, 0))],
        out_shape=[jax.ShapeDtypeStruct((E, rp, D), BF16), jax.ShapeDtypeStruct((E, rs, D), BF16)],
        compiler_params=_cparams(1),
        name="ffn",
    )(xs_p, xs_s, gs_p, gs_s, wg, wu, wd)


def _combine_kernel(pos_ref, out_ref, y_ref, mod_ref, lng_ref, lnb_ref, yo_ref, *, N, cap):
    slot_id = lax.broadcasted_iota(I32, (cap, N), 0)
    onehot = jnp.concatenate(
        [jnp.where(slot_id == pos_ref[0, e:e + 1, :], 1.0, 0.0).astype(BF16) for e in range(N_EXPERTS)], axis=0)
    outs = out_ref[...].reshape(N_EXPERTS * cap, out_ref.shape[2])
    ff = lax.dot_general(onehot, outs, (((0,), (0,)), ((), ())), preferred_element_type=F32)
    yo_ref[0] = _layer_norm(ALPHA * y_ref[0] + mod_ref[0, 5:6, :] * ff, lng_ref[...], lnb_ref[...])


def _combine(pos, outs, y, mod, lng, lnb, per_batch):
    B, N, D = y.shape
    cap = (CAP_FACTOR * N) // N_EXPERTS
    return pl.pallas_call(
        functools.partial(_combine_kernel, N=N, cap=cap),
        grid=(B,),
        in_specs=[pl.BlockSpec((1, N_EXPERTS, N), lambda b: (b, 0, 0)),
                  pl.BlockSpec((N_EXPERTS, cap, D), lambda b: (0, b, 0)),
                  pl.BlockSpec((1, N, D), lambda b: (b, 0, 0)),
                  _mod_spec(per_batch, 1),
                  pl.BlockSpec((1, D), lambda b: (0, 0)),
                  pl.BlockSpec((1, D), lambda b: (0, 0))],
        out_specs=pl.BlockSpec((1, N, D), lambda b: (b, 0, 0)),
        out_shape=jax.ShapeDtypeStruct((B, N, D), F32),
        compiler_params=_cparams(1),
        name="combine",
    )(pos, outs, y, mod, lng.reshape(1, D), lnb.reshape(1, D))


def _to_heads(x, n_heads):
    B, L, _ = x.shape
    return x.reshape(B, L, n_heads, -1).transpose(0, 2, 1, 3)


def kernel(x_prompt, x_sample, state_hgrn, cache_diff_k, cache_diff_v, cache_win_k, cache_win_v, c, c_ctx, w_mod, b_mod, ln_g, ln_b, w_in_even, w_out_even, hgrn_lb_logits, hgrn_norm_g, diff_lambda, diff_norm_g, w_in_odd, w_out_odd, win_sink, w_router, w_exp_gate, w_exp_up, w_exp_down):
    yp, ys = x_prompt, x_sample
    nb_s = ys.shape[0]
    lb_all = jnp.cumsum(jax.nn.softmax(hgrn_lb_logits.astype(F32), axis=0), axis=0)
    cvec = jnp.zeros((16, D_MODEL), F32).at[:nb_s].set(c).at[nb_s].set(c_ctx)
    mods = _modulation(cvec, w_mod, b_mod).reshape(DEPTH, 16, 6, D_MODEL)
    mods = jnp.pad(mods, ((0, 0), (0, 0), (0, 2), (0, 0)))
    wr_pad = jnp.pad(w_router, ((0, 0), (0, 0), (0, LANES - N_EXPERTS)))
    new_state = new_dk = new_dv = new_wk = new_wv = None
    for l in range(DEPTH):
        mod_p = mods[l, nb_s:nb_s + 1]
        mod_s = mods[l, :nb_s]
        j = l // 2
        if l % 2 == 0:
            lam_init = 0.8 - 0.6 * math.exp(-0.3 * l)
            w_in = w_in_even[j].astype(BF16)
            w_out = w_out_even[j].astype(BF16)
            pp = _inproj(yp, mod_p, w_in, False)
            ps = _inproj(ys, mod_s, w_in, True)
            zero_state = jnp.zeros((yp.shape[0], 2, A_HEADS, A_DK, A_DV), F32)
            oa_p, st_p = _hgrn(pp, lb_all[l], hgrn_norm_g[j], zero_state)
            oa_s, _ = _hgrn(ps, lb_all[l], hgrn_norm_g[j], state_hgrn[:, j])
            ob_p = _diff_attn(pp, diff_lambda[j], diff_norm_g[j], lam_init)
            ob_s = _diff_attn(ps, diff_lambda[j], diff_norm_g[j], lam_init, ctx=(cache_diff_k, cache_diff_v, j))
            mix_p, mix_s = [oa_p, ob_p], [oa_s, ob_s]
            k0 = 3 * A_HEADS * A_DK + 2 * A_HEADS * A_DV + B_HEADS * 2 * B_DH
            new_state = st_p[:, None]
            new_dk = _to_heads(pp[..., k0:k0 + B_HEADS * 2 * B_DH], B_HEADS)[:, None]
            new_dv = _to_heads(pp[..., k0 + B_HEADS * 2 * B_DH:], B_HEADS)[:, None]
        else:
            w_in = w_in_odd[j].astype(BF16)
            w_out = w_out_odd[j].astype(BF16)
            pp = _inproj(yp, mod_p, w_in, False)
            ps = _inproj(ys, mod_s, w_in, True)
            mix_p = [_gqa(pp, win_sink[j])]
            mix_s = [_gqa(ps, win_sink[j], ctx=(cache_win_k, cache_win_v, j))]
            k0 = C_HEADS * C_DH
            new_wk = _to_heads(pp[..., k0:k0 + C_KV_HEADS * C_DH], C_KV_HEADS)[:, None]
            new_wv = _to_heads(pp[..., k0 + C_KV_HEADS * C_DH:], C_KV_HEADS)[:, None]
        yp, h2p, lgp = _outproj(yp, mix_p, mod_p, w_out, ln_g[l, 0], ln_b[l, 0], wr_pad[l], False)
        ys, h2s, lgs = _outproj(ys, mix_s, mod_s, w_out, ln_g[l, 0], ln_b[l, 0], wr_pad[l], True)
        pos_p, gs_p = _route(lgp)
        pos_s, gs_s = _route(lgs)
        xs_p = _gather(pos_p, h2p)
        xs_s = _gather(pos_s, h2s)
        out_p, out_s = _ffn(xs_p, xs_s, gs_p, gs_s, w_exp_gate, w_exp_up, w_exp_down, l)
        yp = _combine(pos_p, out_p, yp, mod_p, ln_g[l, 1], ln_b[l, 1], False)
        ys = _combine(pos_s, out_s, ys, mod_s, ln_g[l, 1], ln_b[l, 1], True)
    return (yp, ys, new_state, new_dk, new_dv, new_wk, new_wv)
```

```python
import functools
import math

import jax
import jax.numpy as jnp
from jax import lax
from jax.experimental import pallas as pl
from jax.experimental.pallas import tpu as pltpu

F32 = jnp.float32
BF16 = jnp.bfloat16
I32 = jnp.int32

D_MODEL = 1024
DEPTH = 2
GRID_W = 64
A_HEADS = 4
A_DK = 128
A_DV = 128
HGRN_CHUNK = 16
B_HEADS = 4
B_DH = 64
C_HEADS = 16
C_KV_HEADS = 4
C_GROUP = C_HEADS // C_KV_HEADS
C_DH = 64
WINDOW = 128
N_EXPERTS = 16
CAP_FACTOR = 2
ALPHA = (2 * DEPTH) ** 0.25
LN_EPS = 1e-5
ROPE_BASE = 10000.0
NEG_INF = -1e30
LANES = 128
VMEM_LIMIT = 56 * 1024 * 1024


def _cparams(n_axes):
    return pltpu.CompilerParams(dimension_semantics=("arbitrary",) * n_axes, vmem_limit_bytes=VMEM_LIMIT)


def _dot(a, b):
    return jnp.dot(a.astype(BF16), b.astype(BF16), preferred_element_type=F32)


def _dot_nt(a, b):
    return lax.dot_general(a.astype(BF16), b.astype(BF16), (((1,), (1,)), ((), ())), preferred_element_type=F32)


def _dot_tn(a, b):
    return lax.dot_general(a.astype(BF16), b.astype(BF16), (((0,), (0,)), ((), ())), preferred_element_type=F32)


def _split3(x):
    x1 = x.astype(BF16)
    r1 = x - x1.astype(F32)
    x2 = r1.astype(BF16)
    x3 = (r1 - x2.astype(F32)).astype(BF16)
    return x1, x2, x3


def _dot_sel(m, x):
    x1, x2, x3 = _split3(x)
    return (jnp.dot(m, x1, preferred_element_type=F32) + jnp.dot(m, x2, preferred_element_type=F32)
            + jnp.dot(m, x3, preferred_element_type=F32))


def _mod_kernel(c_ref, w_ref, b_ref, o_ref):
    a = jax.nn.silu(c_ref[...])
    o_ref[0] = jnp.dot(a, w_ref[0], precision=lax.Precision.HIGHEST, preferred_element_type=F32) + b_ref[0]


def _modulation(cvec, w_mod, b_mod):
    n, d = cvec.shape
    tn = 1024
    nt = w_mod.shape[2] // tn
    return pl.pallas_call(
        _mod_kernel,
        grid=(DEPTH, nt),
        in_specs=[pl.BlockSpec((n, d), lambda l, j: (0, 0)),
                  pl.BlockSpec((1, d, tn), lambda l, j: (l, 0, j)),
                  pl.BlockSpec((1, 1, tn), lambda l, j: (l, 0, j))],
        out_specs=pl.BlockSpec((1, n, tn), lambda l, j: (l, 0, j)),
        out_shape=jax.ShapeDtypeStruct((DEPTH, n, w_mod.shape[2]), F32),
        compiler_params=_cparams(2),
        name="modulation",
    )(cvec, w_mod, b_mod.reshape(DEPTH, 1, -1))


def _mod_spec(per_batch, n_axes):
    if n_axes == 1:
        return pl.BlockSpec((1, 8, D_MODEL), (lambda b: (b, 0, 0)) if per_batch else (lambda b: (0, 0, 0)))
    return pl.BlockSpec((1, 8, D_MODEL), (lambda b, i: (b, 0, 0)) if per_batch else (lambda b, i: (0, 0, 0)))


def _inproj_kernel(x_ref, mod_ref, w_ref, o_ref):
    h = x_ref[0] * (1.0 + mod_ref[0, 1:2, :]) + mod_ref[0, 0:1, :]
    o_ref[0] = jnp.dot(h.astype(BF16), w_ref[...], preferred_element_type=F32)


def _inproj(x, mod, w_bf16, per_batch):
    B, L, D = x.shape
    N = w_bf16.shape[1]
    tm = 256
    return pl.pallas_call(
        _inproj_kernel,
        grid=(B, L // tm),
        in_specs=[pl.BlockSpec((1, tm, D), lambda b, i: (b, i, 0)),
                  _mod_spec(per_batch, 2),
                  pl.BlockSpec((D, N), lambda b, i: (0, 0))],
        out_specs=pl.BlockSpec((1, tm, N), lambda b, i: (b, i, 0)),
        out_shape=jax.ShapeDtypeStruct((B, L, N), F32),
        compiler_params=_cparams(2),
        name="inproj",
    )(x, mod, w_bf16)


HGRN_HEADS_PER_STEP = 2


def _hgrn_block(d, z, q, v, lb, st, sums_b, tri, rowchunk):
    blk = LANES
    nch = blk // HGRN_CHUNK
    logf = jnp.log(lb + (1.0 - lb) * jax.nn.sigmoid(z))
    key = (1.0 - lb) * jax.nn.sigmoid(-z)
    sums = jnp.dot(sums_b, jnp.concatenate(_split3(logf), axis=1), preferred_element_type=F32)
    sums = sums[:, 0:A_DK] + sums[:, A_DK:2 * A_DK] + sums[:, 2 * A_DK:3 * A_DK]
    b, tot = sums[0:blk], sums[blk:2 * blk]
    qd = (q * jnp.exp(b)).astype(BF16)
    kd = (key * jnp.exp(-b)).astype(BF16)
    ke = key * jnp.exp(tot - b)
    scores = jnp.where(tri, _dot_nt(qd, kd), 0.0)
    o = _dot(scores, v)
    ke_exp = jnp.concatenate([jnp.where(rowchunk == cc, ke, 0.0).astype(BF16) for cc in range(nch)], axis=1)
    ut = jnp.dot(v.T.astype(BF16), ke_exp, preferred_element_type=F32)
    dec = jnp.exp(tot)
    outs = [None] * nch
    for cc in (range(nch) if d == 0 else reversed(range(nch))):
        lo = cc * HGRN_CHUNK
        outs[cc] = _dot_nt(qd[lo:lo + HGRN_CHUNK], st)
        st = st * dec[lo:lo + 1, :] + ut[:, cc * A_DK:(cc + 1) * A_DK]
    return o + jnp.concatenate(outs, axis=0), st


def _hgrn_kernel(q_ref, zf_ref, zb_ref, v_ref, g_ref, lb_ref, gn_ref, s0_ref, o_ref, sfin_ref, obw_scr, *, nblk):
    blk = LANES
    hps = HGRN_HEADS_PER_STEP
    gn = gn_ref[...]
    r = lax.broadcasted_iota(I32, (blk, blk), 0)
    c = lax.broadcasted_iota(I32, (blk, blk), 1)
    same = (r // HGRN_CHUNK) == (c // HGRN_CHUNK)
    rowchunk = r // HGRN_CHUNK
    ones_f = jnp.where(same, 1.0, 0.0)
    tri = [jnp.logical_and(same, c <= r), jnp.logical_and(same, c >= r)]
    sums_b = [jnp.concatenate([jnp.where(t, 1.0, 0.0), ones_f], axis=0).astype(BF16) for t in tri]
    z_refs = (zf_ref, zb_ref)

    def body(i, sts):
        new = []
        for hh in range(hps):
            cols = slice(hh * LANES, (hh + 1) * LANES)
            lb = lb_ref[hh]
            for d in (0, 1):
                bi = i if d == 0 else nblk - 1 - i
                rows = pl.ds(pl.multiple_of(bi * blk, blk), blk)
                o, st = _hgrn_block(d, z_refs[d][0, rows, cols], q_ref[0, rows, cols], v_ref[0, rows, cols], lb,
                                    sts[hh * 2 + d], sums_b[d], tri[d], rowchunk)
                if d == 0:
                    o_ref[0, rows, cols] = o
                else:
                    obw_scr[rows, cols] = o
                new.append(st)
        return tuple(new)

    init = tuple(s0_ref[0, d, hh].T for hh in range(hps) for d in (0, 1))
    sts = lax.fori_loop(0, nblk, body, init)
    for hh in range(hps):
        for d in (0, 1):
            sfin_ref[0, d, hh] = sts[hh * 2 + d].T

    def finish(i, carry):
        rows = pl.ds(pl.multiple_of(i * blk, blk), blk)
        for hh in range(hps):
            cols = slice(hh * LANES, (hh + 1) * LANES)
            t = o_ref[0, rows, cols] + obw_scr[rows, cols]
            ms = jnp.mean(jnp.square(t), axis=-1, keepdims=True)
            o_ref[0, rows, cols] = t * lax.rsqrt(ms + LN_EPS) * gn * jax.nn.silu(g_ref[0, rows, cols])
        return carry

    lax.fori_loop(0, nblk, finish, 0)


def _hgrn(p, lb, gn, s0):
    B, L, _ = p.shape
    H = A_HEADS
    hps = HGRN_HEADS_PER_STEP
    w = hps * LANES

    def col(off):
        return pl.BlockSpec((1, L, w), lambda b, h, off=off: (b, 0, off + h))

    st_spec = pl.BlockSpec((1, 2, hps, A_DK, A_DV), lambda b, h: (b, 0, h, 0, 0))
    ng = H // hps
    return pl.pallas_call(
        functools.partial(_hgrn_kernel, nblk=L // LANES),
        grid=(B, ng),
        in_specs=[col(0), col(ng), col(2 * ng), col(3 * ng), col(4 * ng),
                  pl.BlockSpec((hps, 1, A_DK), lambda b, h: (h, 0, 0)),
                  pl.BlockSpec((1, A_DV), lambda b, h: (0, 0)),
                  st_spec],
        out_specs=[pl.BlockSpec((1, L, w), lambda b, h: (b, 0, h)), st_spec],
        out_shape=[jax.ShapeDtypeStruct((B, L, H * A_DV), F32), jax.ShapeDtypeStruct((B, 2, H, A_DK, A_DV), F32)],
        scratch_shapes=[pltpu.VMEM((L, w), F32)],
        compiler_params=_cparams(2),
        name="hgrn",
    )(p, p, p, p, p, lb.reshape(H, 1, A_DK), gn.reshape(1, A_DV), s0)


def _rope_tables(L, width):
    quarter = C_DH // 4
    t = jnp.arange(L)
    rows = (t // GRID_W).astype(F32)
    cols = (t % GRID_W).astype(F32)
    inv = ROPE_BASE ** (-jnp.arange(quarter, dtype=F32) / quarter)
    ang_r = rows[:, None] * inv[None, :]
    ang_c = cols[:, None] * inv[None, :]
    cos = jnp.concatenate([jnp.cos(ang_r), jnp.cos(ang_r), jnp.cos(ang_c), jnp.cos(ang_c)], -1)
    sin = jnp.concatenate([-jnp.sin(ang_r), jnp.sin(ang_r), -jnp.sin(ang_c), jnp.sin(ang_c)], -1)
    reps = width // C_DH
    return jnp.tile(cos, (1, reps)), jnp.tile(sin, (1, reps))


def _rope(x, cos, sin):
    n = x.shape[-1]
    q = C_DH // 4
    lane = lax.broadcasted_iota(I32, x.shape, x.ndim - 1)
    partner = jnp.where((lane & q) == 0, pltpu.roll(x, n - q, axis=x.ndim - 1), pltpu.roll(x, q, axis=x.ndim - 1))
    return x * cos + partner * sin


def _diff_kernel(*refs, L, n_ctx, lam_init, tq):
    if n_ctx:
        q_ref, k_ref, v_ref, lam_ref, gn_ref, kc_ref, vc_ref, cos_ref, sin_ref, o_ref, kk_scr, vv_scr = refs
    else:
        q_ref, k_ref, v_ref, lam_ref, gn_ref, o_ref, kk_scr, vv_scr = refs
    lam = lam_ref[...]
    lam_full = (jnp.exp(jnp.sum(lam[0:1] * lam[1:2], keepdims=True)) - jnp.exp(jnp.sum(lam[2:3] * lam[3:4], keepdims=True))
                + lam_init)
    gn = gn_ref[...]
    if n_ctx:
        kk_scr[0:n_ctx, :] = kc_ref[0, 0, 0].astype(BF16)
        vv_scr[0:n_ctx, :] = vc_ref[0, 0, 0].astype(BF16)
        kk_scr[n_ctx:, :] = _rope(k_ref[0], cos_ref[...], sin_ref[...]).astype(BF16)
    else:
        kk_scr[...] = k_ref[0].astype(BF16)
    vv_scr[n_ctx:, :] = v_ref[0].astype(BF16)
    scale = B_DH ** -0.5

    def body(i, carry):
        rows = pl.ds(pl.multiple_of(i * tq, tq), tq)
        q = q_ref[0, rows, :]
        if n_ctx:
            q = _rope(q, cos_ref[rows, :], sin_ref[rows, :])
        qb = (q * scale).astype(BF16)
        kk = kk_scr[...]
        vv = vv_scr[...]
        outs = []
        for part in (0, 1):
            sl = slice(part * B_DH, (part + 1) * B_DH)
            s = lax.dot_general(qb[:, sl], kk[:, sl], (((1,), (1,)), ((), ())), preferred_element_type=F32)
            m = jnp.max(s, axis=-1, keepdims=True)
            e = jnp.exp(s - m)
            den = jnp.sum(e, axis=-1, keepdims=True)
            outs.append(jnp.dot(e.astype(BF16), vv, preferred_element_type=F32) / den)
        o = outs[0] - lam_full * outs[1]
        ms = jnp.mean(jnp.square(o), axis=-1, keepdims=True)
        o_ref[0, rows, :] = o * lax.rsqrt(ms + LN_EPS) * gn * (1.0 - lam_init)
        return carry

    lax.fori_loop(0, L // tq, body, 0)


def _diff_attn(p, lam, gn, lam_init, ctx=None):
    B, L, _ = p.shape
    H = B_HEADS
    off = 5 * A_HEADS

    def col(o):
        return pl.BlockSpec((1, L, LANES), lambda b, h, o=o: (b, 0, o + h))

    in_specs = [col(off), col(off + H), col(off + 2 * H),
                pl.BlockSpec(lam.shape, lambda b, h: (0, 0)),
                pl.BlockSpec((1, LANES), lambda b, h: (0, 0))]
    args = [p, p, p, lam, gn.reshape(1, LANES)]
    n_ctx = 0
    if ctx is not None:
        kc, vc, j = ctx
        n_ctx = kc.shape[3]
        cos, sin = _rope_tables(L, LANES)
        in_specs += [pl.BlockSpec((1, 1, 1, n_ctx, LANES), lambda b, h, j=j: (b, j, h, 0, 0)),
                     pl.BlockSpec((1, 1, 1, n_ctx, LANES), lambda b, h, j=j: (b, j, h, 0, 0)),
                     pl.BlockSpec((L, LANES), lambda b, h: (0, 0)),
                     pl.BlockSpec((L, LANES), lambda b, h: (0, 0))]
        args += [kc, vc, cos, sin]
    return pl.pallas_call(
        functools.partial(_diff_kernel, L=L, n_ctx=n_ctx, lam_init=lam_init, tq=min(L, 256)),
        grid=(B, H),
        in_specs=in_specs,
        out_specs=pl.BlockSpec((1, L, LANES), lambda b, h: (b, 0, h)),
        out_shape=jax.ShapeDtypeStruct((B, L, H * LANES), F32),
        scratch_shapes=[pltpu.VMEM((n_ctx + L, LANES), BF16), pltpu.VMEM((n_ctx + L, LANES), BF16)],
        compiler_params=_cparams(2),
        name="diff_attn",
    )(*args)


def _sink_col(sink_ref, hk, rows):
    return jnp.concatenate([jnp.full((rows, 1), sink_ref[hk * C_GROUP + g], F32) for g in range(C_GROUP)], axis=0)


def _gqa_full_kernel(sink_ref, q_ref, k_ref, v_ref, o_ref, *, L):
    hp = pl.program_id(1)
    scale = C_DH ** -0.5
    q = q_ref[0] * scale
    kb = k_ref[0].astype(BF16)
    vb = v_ref[0].astype(BF16)
    for kh in (0, 1):
        hk = hp * 2 + kh
        ksl = slice(kh * C_DH, (kh + 1) * C_DH)
        q4 = jnp.concatenate([q[:, (kh * C_GROUP + g) * C_DH:(kh * C_GROUP + g + 1) * C_DH] for g in range(C_GROUP)], axis=0)
        s = lax.dot_general(q4.astype(BF16), kb[:, ksl], (((1,), (1,)), ((), ())), preferred_element_type=F32)
        sink = _sink_col(sink_ref, hk, L)
        m = jnp.maximum(jnp.max(s, axis=-1, keepdims=True), sink)
        e = jnp.exp(s - m)
        den = jnp.sum(e, axis=-1, keepdims=True) + jnp.exp(sink - m)
        o4 = jnp.dot(e.astype(BF16), vb[:, ksl], preferred_element_type=F32) / den
        for g in range(C_GROUP):
            c0 = (kh * C_GROUP + g) * C_DH
            o_ref[0, :, c0:c0 + C_DH] = o4[g * L:(g + 1) * L]


def _gqa_win_kernel(sink_ref, q_ref, k_ref, v_ref, kc_ref, vc_ref, cosq_ref, sinq_ref, o_ref, kr_scr, *, L):
    hp = pl.program_id(1)
    tq = WINDOW
    nloc = 3 * tq
    scale = C_DH ** -0.5
    kr_scr[...] = _rope(k_ref[0], cosq_ref[:, 0:LANES], sinq_ref[:, 0:LANES]).astype(BF16)

    def body(n, carry):
        r0 = pl.multiple_of(n * tq, tq)
        start = pl.multiple_of(jnp.clip((n - 1) * tq, 0, L - nloc), tq)
        rows = pl.ds(r0, tq)
        q = _rope(q_ref[0, rows, :], cosq_ref[rows, :], sinq_ref[rows, :]) * scale
        kl = kr_scr[pl.ds(start, nloc), :]
        vl = v_ref[0, pl.ds(start, nloc), :].astype(BF16)
        qpos = r0 + (lax.broadcasted_iota(I32, (C_GROUP * tq, nloc), 0) % tq)
        kpos = start + lax.broadcasted_iota(I32, (C_GROUP * tq, nloc), 1)
        valid = jnp.abs(kpos - qpos) <= WINDOW
        for kh in (0, 1):
            hk = hp * 2 + kh
            ksl = slice(kh * C_DH, (kh + 1) * C_DH)
            q4 = jnp.concatenate([q[:, (kh * C_GROUP + g) * C_DH:(kh * C_GROUP + g + 1) * C_DH] for g in range(C_GROUP)],
                                 axis=0).astype(BF16)
            s_loc = lax.dot_general(q4, kl[:, ksl], (((1,), (1,)), ((), ())), preferred_element_type=F32)
            s_loc = jnp.where(valid, s_loc, NEG_INF)
            s_ctx = lax.dot_general(q4, kc_ref[0, 0, kh].astype(BF16), (((1,), (1,)), ((), ())), preferred_element_type=F32)
            sink = _sink_col(sink_ref, hk, tq)
            m = jnp.maximum(jnp.maximum(jnp.max(s_loc, axis=-1, keepdims=True), jnp.max(s_ctx, axis=-1, keepdims=True)), sink)
            e_loc = jnp.exp(s_loc - m)
            e_ctx = jnp.exp(s_ctx - m)
            den = jnp.sum(e_loc, axis=-1, keepdims=True) + jnp.sum(e_ctx, axis=-1, keepdims=True) + jnp.exp(sink - m)
            o4 = (jnp.dot(e_loc.astype(BF16), vl[:, ksl], preferred_element_type=F32)
                  + jnp.dot(e_ctx.astype(BF16), vc_ref[0, 0, kh].astype(BF16), preferred_element_type=F32)) / den
            for g in range(C_GROUP):
                c0 = (kh * C_GROUP + g) * C_DH
                o_ref[0, rows, c0:c0 + C_DH] = o4[g * tq:(g + 1) * tq]
        return carry

    lax.fori_loop(0, L // tq, body, 0)


def _gqa(p, sink, ctx=None):
    B, L, _ = p.shape
    qw = 2 * C_GROUP * C_DH
    nq = C_HEADS * C_DH // LANES
    nk = C_KV_HEADS * C_DH // LANES
    in_specs = [pl.BlockSpec(memory_space=pltpu.SMEM),
                pl.BlockSpec((1, L, qw), lambda b, h: (b, 0, h)),
                pl.BlockSpec((1, L, LANES), lambda b, h: (b, 0, nq + h)),
                pl.BlockSpec((1, L, LANES), lambda b, h: (b, 0, nq + nk + h))]
    args = [sink, p, p, p]
    scratch = []
    if ctx is None:
        body = functools.partial(_gqa_full_kernel, L=L)
    else:
        kc, vc, j = ctx
        n_ctx = kc.shape[3]
        cos, sin = _rope_tables(L, qw)
        in_specs += [pl.BlockSpec((1, 1, 2, n_ctx, C_DH), lambda b, h, j=j: (b, j, h, 0, 0)),
                     pl.BlockSpec((1, 1, 2, n_ctx, C_DH), lambda b, h, j=j: (b, j, h, 0, 0)),
                     pl.BlockSpec((L, qw), lambda b, h: (0, 0)),
                     pl.BlockSpec((L, qw), lambda b, h: (0, 0))]
        args += [kc, vc, cos, sin]
        scratch = [pltpu.VMEM((L, LANES), BF16)]
        body = functools.partial(_gqa_win_kernel, L=L)
    return pl.pallas_call(
        body,
        grid=(B, C_KV_HEADS // 2),
        in_specs=in_specs,
        out_specs=pl.BlockSpec((1, L, qw), lambda b, h: (b, 0, h)),
        out_shape=jax.ShapeDtypeStruct((B, L, C_HEADS * C_DH), F32),
        scratch_shapes=scratch,
        compiler_params=_cparams(2),
        name="gqa",
    )(*args)


def _layer_norm(t, g, b):
    mu = jnp.mean(t, axis=-1, keepdims=True)
    tc = t - mu
    var = jnp.mean(jnp.square(tc), axis=-1, keepdims=True)
    return tc * lax.rsqrt(var + LN_EPS) * g + b


def _outproj_kernel(*refs, n_in):
    y_ref = refs[0]
    a_refs = refs[1:1 + n_in]
    mod_ref, w_ref, lng_ref, lnb_ref, wr_ref, yo_ref, h2_ref, lg_ref = refs[1 + n_in:]
    mix = None
    k0 = 0
    for a_ref in a_refs:
        kw = a_ref.shape[2]
        part = jnp.dot(a_ref[0].astype(BF16), w_ref[k0:k0 + kw, :], preferred_element_type=F32)
        mix = part if mix is None else mix + part
        k0 += kw
    yn = _layer_norm(ALPHA * y_ref[0] + mod_ref[0, 2:3, :] * mix, lng_ref[...], lnb_ref[...])
    yo_ref[0] = yn
    h2 = yn * (1.0 + mod_ref[0, 4:5, :]) + mod_ref[0, 3:4, :]
    h_hi = h2.astype(BF16)
    h2_ref[0] = h_hi
    h_lo = (h2 - h_hi.astype(F32)).astype(BF16)
    both = jnp.dot(h_hi, wr_ref[...], preferred_element_type=F32)
    lg_ref[0] = (both[:, 0:LANES] + both[:, LANES:2 * LANES]
                 + jnp.dot(h_lo, wr_ref[:, 0:LANES], preferred_element_type=F32))


def _outproj(y, mixes, mod, w_bf16, lng, lnb, wr_pad, per_batch):
    B, L, D = y.shape
    tm = 256
    n_in = len(mixes)
    in_specs = [pl.BlockSpec((1, tm, D), lambda b, i: (b, i, 0))]
    in_specs += [pl.BlockSpec((1, tm, m.shape[2]), lambda b, i: (b, i, 0)) for m in mixes]
    in_specs += [_mod_spec(per_batch, 2),
                 pl.BlockSpec(w_bf16.shape, lambda b, i: (0, 0)),
                 pl.BlockSpec((1, D), lambda b, i: (0, 0)),
                 pl.BlockSpec((1, D), lambda b, i: (0, 0)),
                 pl.BlockSpec((D, 2 * LANES), lambda b, i: (0, 0))]
    return pl.pallas_call(
        functools.partial(_outproj_kernel, n_in=n_in),
        grid=(B, L // tm),
        in_specs=in_specs,
        out_specs=[pl.BlockSpec((1, tm, D), lambda b, i: (b, i, 0)),
                   pl.BlockSpec((1, tm, D), lambda b, i: (b, i, 0)),
                   pl.BlockSpec((1, tm, LANES), lambda b, i: (b, i, 0))],
        out_shape=[jax.ShapeDtypeStruct((B, L, D), F32), jax.ShapeDtypeStruct((B, L, D), BF16),
                   jax.ShapeDtypeStruct((B, L, LANES), F32)],
        compiler_params=_cparams(2),
        name="outproj",
    )(y, *mixes, mod, w_bf16, lng.reshape(1, D), lnb.reshape(1, D), wr_pad)


def _route_kernel(lg_ref, upper_ref, pos_ref, gs_ref, aff_scr, thr_scr, *, B, N, cap):
    E = N_EXPERTS

    def affinities(b, carry):
        lg = lg_ref[b]
        lane = lax.broadcasted_iota(I32, lg.shape, 1)
        lgm = jnp.where(lane < E, lg, -jnp.inf)
        m = jnp.max(lgm, axis=-1, keepdims=True)
        ex = jnp.exp(lgm - m)
        aff = ex / jnp.sum(ex, axis=-1, keepdims=True)
        aff_scr[pl.ds(pl.multiple_of(b * E, E), E), :] = aff.T[0:E, :]
        return carry

    lax.fori_loop(0, B, affinities, 0)
    bits = pltpu.bitcast(aff_scr[...], I32)

    def bisect(_, carry):
        lo, hi = carry
        mid = lo + jnp.right_shift(hi - lo + 1, 1)
        cnt = jnp.sum((bits >= mid).astype(I32), axis=-1, keepdims=True)
        ok = cnt >= cap
        return jnp.where(ok, mid, lo), jnp.where(ok, hi, mid - 1)

    lo0 = jnp.zeros((B * E, 1), I32)
    hi0 = jnp.full((B * E, 1), 0x7F800000, I32)
    thr, _ = lax.fori_loop(0, 31, bisect, (lo0, hi0))
    thr_scr[...] = jnp.broadcast_to(thr, (B * E, LANES))

    def select(b, carry):
        rows = pl.ds(pl.multiple_of(b * E, E), E)
        aff_t = aff_scr[rows, :]
        bits_b = pltpu.bitcast(aff_t, I32)
        thr_b = thr_scr[rows, 0:1]
        gt = bits_b > thr_b
        eq = bits_b == thr_b
        need = cap - jnp.sum(gt.astype(I32), axis=-1, keepdims=True)
        upper = upper_ref[...]
        tie_rank = jnp.dot(jnp.where(eq, 1.0, 0.0).astype(BF16), upper, preferred_element_type=F32)
        sel = jnp.logical_or(gt, jnp.logical_and(eq, tie_rank < need.astype(F32)))
        slot = jnp.dot(jnp.where(sel, 1.0, 0.0).astype(BF16), upper, preferred_element_type=F32)
        pos = jnp.where(sel, slot.astype(I32), -1)
        pos_ref[b] = pos
        slot_id = lax.broadcasted_iota(I32, (cap, N), 0)
        for e in range(E):
            hit = slot_id == pos[e:e + 1, :]
            gcol = jnp.sum(jnp.where(hit, aff_t[e:e + 1, :], 0.0), axis=-1, keepdims=True)
            gs_ref[b, e] = jnp.broadcast_to(gcol, (cap, LANES))
        return carry

    lax.fori_loop(0, B, select, 0)


def _route(logits):
    B, N, _ = logits.shape
    E = N_EXPERTS
    cap = (CAP_FACTOR * N) // E
    upper = (jnp.arange(N)[:, None] < jnp.arange(N)[None, :]).astype(BF16)
    return pl.pallas_call(
        functools.partial(_route_kernel, B=B, N=N, cap=cap),
        grid=(1,),
        in_specs=[pl.BlockSpec((B, N, LANES), lambda i: (0, 0, 0)),
                  pl.BlockSpec((N, N), lambda i: (0, 0))],
        out_specs=[pl.BlockSpec((B, E, N), lambda i: (0, 0, 0)),
                   pl.BlockSpec((B, E, cap, LANES), lambda i: (0, 0, 0, 0))],
        out_shape=[jax.ShapeDtypeStruct((B, E, N), I32),
                   jax.ShapeDtypeStruct((B, E, cap, LANES), F32)],
        scratch_shapes=[pltpu.VMEM((B * E, N), F32), pltpu.VMEM((B * E, LANES), I32)],
        compiler_params=_cparams(1),
        name="route",
    )(logits, upper)


def _gather_kernel(pos_ref, h_ref, xs_ref, *, N, cap):
    h = h_ref[0]
    slot_id = lax.broadcasted_iota(I32, (cap, N), 0)
    for e in range(N_EXPERTS):
        onehot = jnp.where(slot_id == pos_ref[0, e:e + 1, :], 1.0, 0.0).astype(BF16)
        xs_ref[e] = jnp.dot(onehot, h, preferred_element_type=F32).astype(BF16)


def _gather(pos, h2):
    B, N, D = h2.shape
    cap = (CAP_FACTOR * N) // N_EXPERTS
    return pl.pallas_call(
        functools.partial(_gather_kernel, N=N, cap=cap),
        grid=(B,),
        in_specs=[pl.BlockSpec((1, N_EXPERTS, N), lambda b: (b, 0, 0)),
                  pl.BlockSpec((1, N, D), lambda b: (b, 0, 0))],
        out_specs=pl.BlockSpec((N_EXPERTS, cap, D), lambda b: (0, b, 0)),
        out_shape=jax.ShapeDtypeStruct((N_EXPERTS, B * cap, D), BF16),
        compiler_params=_cparams(1),
        name="gather",
    )(pos, h2)


def _ffn_kernel(xp_ref, xs_ref, gp_ref, gs_ref, wg_ref, wu_ref, wd_ref, op_ref, os_ref, *, tr):
    wg = wg_ref[0, 0].astype(BF16)
    wu = wu_ref[0, 0].astype(BF16)
    wd = wd_ref[0, 0].astype(BF16)
    for x_ref, g_ref, o_ref in ((xp_ref, gp_ref, op_ref), (xs_ref, gs_ref, os_ref)):
        rows = x_ref.shape[1]
        cap = g_ref.shape[2]
        for r0 in range(0, rows, tr):
            x = x_ref[0, r0:r0 + tr, :]
            hid = jax.nn.silu(jnp.dot(x, wg, preferred_element_type=F32)) * jnp.dot(x, wu, preferred_element_type=F32)
            out = jnp.dot(hid.astype(BF16), wd, preferred_element_type=F32)
            gate = g_ref[r0 // cap:(r0 + tr) // cap, 0].reshape(tr, LANES)[:, 0:1]
            o_ref[0, r0:r0 + tr, :] = (out * gate).astype(BF16)


def _ffn(xs_p, xs_s, gs_p, gs_s, wg, wu, wd, l):
    E, rp, D = xs_p.shape
    rs = xs_s.shape[1]
    FF = wg.shape[3]
    tr = 512

    def gspec(g):
        return pl.BlockSpec((g.shape[0], 1, g.shape[2], LANES), lambda e: (0, e, 0, 0))

    return pl.pallas_call(
        functools.partial(_ffn_kernel, tr=tr),
        grid=(E,),
        in_specs=[pl.BlockSpec((1, rp, D), lambda e: (e, 0, 0)),
                  pl.BlockSpec((1, rs, D), lambda e: (e, 0, 0)),
                  gspec(gs_p), gspec(gs_s),
                  pl.BlockSpec((1, 1, D, FF), lambda e, l=l: (l, e, 0, 0)),
                  pl.BlockSpec((1, 1, D, FF), lambda e, l=l: (l, e, 0, 0)),
                  pl.BlockSpec((1, 1, FF, D), lambda e, l=l: (l, e, 0, 0))],
        out_specs=[pl.BlockSpec((1, rp, D), lambda e: (e, 0, 0)),
                   pl.BlockSpec((1, rs, D), lambda e: (e, 0, 0))],
        out_shape=[jax.ShapeDtypeStruct((E, rp, D), BF16), jax.ShapeDtypeStruct((E, rs, D), BF16)],
        compiler_params=_cparams(1),
        name="ffn",
    )(xs_p, xs_s, gs_p, gs_s, wg, wu, wd)


def _combine_kernel(pos_ref, out_ref, y_ref, mod_ref, lng_ref, lnb_ref, yo_ref, *, N, cap):
    slot_id = lax.broadcasted_iota(I32, (cap, N), 0)
    onehot = jnp.concatenate(
        [jnp.where(slot_id == pos_ref[0, e:e + 1, :], 1.0, 0.0).astype(BF16) for e in range(N_EXPERTS)], axis=0)
    outs = out_ref[...].reshape(N_EXPERTS * cap, out_ref.shape[2])
    ff = lax.dot_general(onehot, outs, (((0,), (0,)), ((), ())), preferred_element_type=F32)
    yo_ref[0] = _layer_norm(ALPHA * y_ref[0] + mod_ref[0, 5:6, :] * ff, lng_ref[...], lnb_ref[...])


def _combine(pos, outs, y, mod, lng, lnb, per_batch):
    B, N, D = y.shape
    cap = (CAP_FACTOR * N) // N_EXPERTS
    return pl.pallas_call(
        functools.partial(_combine_kernel, N=N, cap=cap),
        grid=(B,),
        in_specs=[pl.BlockSpec((1, N_EXPERTS, N), lambda b: (b, 0, 0)),
                  pl.BlockSpec((N_EXPERTS, cap, D), lambda b: (0, b, 0)),
                  pl.BlockSpec((1, N, D), lambda b: (b, 0, 0)),
                  _mod_spec(per_batch, 1),
                  pl.BlockSpec((1, D), lambda b: (0, 0)),
                  pl.BlockSpec((1, D), lambda b: (0, 0))],
        out_specs=pl.BlockSpec((1, N, D), lambda b: (b, 0, 0)),
        out_shape=jax.ShapeDtypeStruct((B, N, D), F32),
        compiler_params=_cparams(1),
        name="combine",
    )(pos, outs, y, mod, lng.reshape(1, D), lnb.reshape(1, D))


def _to_heads(x, n_heads):
    B, L, _ = x.shape
    return x.reshape(B, L, n_heads, -1).transpose(0, 2, 1, 3)


def kernel(x_prompt, x_sample, state_hgrn, cache_diff_k, cache_diff_v, cache_win_k, cache_win_v, c, c_ctx, w_mod, b_mod, ln_g, ln_b, w_in_even, w_out_even, hgrn_lb_logits, hgrn_norm_g, diff_lambda, diff_norm_g, w_in_odd, w_out_odd, win_sink, w_router, w_exp_gate, w_exp_up, w_exp_down):
    yp, ys = x_prompt, x_sample
    nb_s = ys.shape[0]
    lb_all = jnp.cumsum(jax.nn.softmax(hgrn_lb_logits.astype(F32), axis=0), axis=0)
    cvec = jnp.zeros((16, D_MODEL), F32).at[:nb_s].set(c).at[nb_s].set(c_ctx)
    mods = _modulation(cvec, w_mod, b_mod).reshape(DEPTH, 16, 6, D_MODEL)
    mods = jnp.pad(mods, ((0, 0), (0, 0), (0, 2), (0, 0)))
    wr_f32 = jnp.pad(w_router, ((0, 0), (0, 0), (0, LANES - N_EXPERTS)))
    wr_hi = wr_f32.astype(BF16)
    wr_pad = jnp.concatenate([wr_hi, (wr_f32 - wr_hi.astype(F32)).astype(BF16)], axis=-1)
    new_state = new_dk = new_dv = new_wk = new_wv = None
    for l in range(DEPTH):
        mod_p = mods[l, nb_s:nb_s + 1]
        mod_s = mods[l, :nb_s]
        j = l // 2
        if l % 2 == 0:
            lam_init = 0.8 - 0.6 * math.exp(-0.3 * l)
            w_in = w_in_even[j].astype(BF16)
            w_out = w_out_even[j].astype(BF16)
            pp = _inproj(yp, mod_p, w_in, False)
            ps = _inproj(ys, mod_s, w_in, True)
            zero_state = jnp.zeros((yp.shape[0], 2, A_HEADS, A_DK, A_DV), F32)
            oa_p, st_p = _hgrn(pp, lb_all[l], hgrn_norm_g[j], zero_state)
            oa_s, _ = _hgrn(ps, lb_all[l], hgrn_norm_g[j], state_hgrn[:, j])
            ob_p = _diff_attn(pp, diff_lambda[j], diff_norm_g[j], lam_init)
            ob_s = _diff_attn(ps, diff_lambda[j], diff_norm_g[j], lam_init, ctx=(cache_diff_k, cache_diff_v, j))
            mix_p, mix_s = [oa_p, ob_p], [oa_s, ob_s]
            k0 = 3 * A_HEADS * A_DK + 2 * A_HEADS * A_DV + B_HEADS * 2 * B_DH
            new_state = st_p[:, None]
            new_dk = _to_heads(pp[..., k0:k0 + B_HEADS * 2 * B_DH], B_HEADS)[:, None]
            new_dv = _to_heads(pp[..., k0 + B_HEADS * 2 * B_DH:], B_HEADS)[:, None]
        else:
            w_in = w_in_odd[j].astype(BF16)
            w_out = w_out_odd[j].astype(BF16)
            pp = _inproj(yp, mod_p, w_in, False)
            ps = _inproj(ys, mod_s, w_in, True)
            mix_p = [_gqa(pp, win_sink[j])]
            mix_s = [_gqa(ps, win_sink[j], ctx=(cache_win_k, cache_win_v, j))]
            k0 = C_HEADS * C_DH
            new_wk = _to_heads(pp[..., k0:k0 + C_KV_HEADS * C_DH], C_KV_HEADS)[:, None]
            new_wv = _to_heads(pp[..., k0 + C_KV_HEADS * C_DH:], C_KV_HEADS)[:, None]
        yp, h2p, lgp = _outproj(yp, mix_p, mod_p, w_out, ln_g[l, 0], ln_b[l, 0], wr_pad[l], False)
        ys, h2s, lgs = _outproj(ys, mix_s, mod_s, w_out, ln_g[l, 0], ln_b[l, 0], wr_pad[l], True)
        pos_p, gs_p = _route(lgp)
        pos_s, gs_s = _route(lgs)
        xs_p = _gather(pos_p, h2p)
        xs_s = _gather(pos_s, h2s)
        out_p, out_s = _ffn(xs_p, xs_s, gs_p, gs_s, w_exp_gate, w_exp_up, w_exp_down, l)
        yp = _combine(pos_p, out_p, yp, mod_p, ln_g[l, 1], ln_b[l, 1], False)
        ys = _combine(pos_s, out_s, ys, mod_s, ln_g[l, 1], ln_b[l, 1], True)
    return (yp, ys, new_state, new_dk, new_dv, new_wk, new_wv)
```

```python
import functools
import math

import jax
import jax.numpy as jnp
from jax import lax
from jax.experimental import pallas as pl
from jax.experimental.pallas import tpu as pltpu

F32 = jnp.float32
BF16 = jnp.bfloat16
I32 = jnp.int32

D_MODEL = 1024
DEPTH = 2
GRID_W = 64
A_HEADS = 4
A_DK = 128
A_DV = 128
HGRN_CHUNK = 16
B_HEADS = 4
B_DH = 64
C_HEADS = 16
C_KV_HEADS = 4
C_GROUP = C_HEADS // C_KV_HEADS
C_DH = 64
WINDOW = 128
N_EXPERTS = 16
CAP_FACTOR = 2
ALPHA = (2 * DEPTH) ** 0.25
LN_EPS = 1e-5
ROPE_BASE = 10000.0
NEG_INF = -1e30
LOG2E = math.log2(math.e)
LANES = 128
VMEM_LIMIT = 56 * 1024 * 1024


def _cparams(n_axes):
    return pltpu.CompilerParams(dimension_semantics=("arbitrary",) * n_axes, vmem_limit_bytes=VMEM_LIMIT)


def _dot(a, b):
    return jnp.dot(a.astype(BF16), b.astype(BF16), preferred_element_type=F32)


def _dot_nt(a, b):
    return lax.dot_general(a.astype(BF16), b.astype(BF16), (((1,), (1,)), ((), ())), preferred_element_type=F32)


def _dot_tn(a, b):
    return lax.dot_general(a.astype(BF16), b.astype(BF16), (((0,), (0,)), ((), ())), preferred_element_type=F32)


def _split3(x):
    x1 = x.astype(BF16)
    r1 = x - x1.astype(F32)
    x2 = r1.astype(BF16)
    x3 = (r1 - x2.astype(F32)).astype(BF16)
    return x1, x2, x3


def _dot_sel(m, x):
    x1, x2, x3 = _split3(x)
    return (jnp.dot(m, x1, preferred_element_type=F32) + jnp.dot(m, x2, preferred_element_type=F32)
            + jnp.dot(m, x3, preferred_element_type=F32))


def _mod_kernel(c_ref, w_ref, b_ref, o_ref):
    a = jax.nn.silu(c_ref[...])
    w = w_ref[0]
    a_hi = a.astype(BF16)
    a_lo = (a - a_hi.astype(F32)).astype(BF16)
    w_hi = w.astype(BF16)
    w_lo = (w - w_hi.astype(F32)).astype(BF16)
    o_ref[0] = (jnp.dot(a_hi, w_hi, preferred_element_type=F32) + jnp.dot(a_hi, w_lo, preferred_element_type=F32)
                + jnp.dot(a_lo, w_hi, preferred_element_type=F32) + b_ref[0])


def _modulation(cvec, w_mod, b_mod):
    n, d = cvec.shape
    tn = 1024
    nt = w_mod.shape[2] // tn
    return pl.pallas_call(
        _mod_kernel,
        grid=(DEPTH, nt),
        in_specs=[pl.BlockSpec((n, d), lambda l, j: (0, 0)),
                  pl.BlockSpec((1, d, tn), lambda l, j: (l, 0, j)),
                  pl.BlockSpec((1, 1, tn), lambda l, j: (l, 0, j))],
        out_specs=pl.BlockSpec((1, n, tn), lambda l, j: (l, 0, j)),
        out_shape=jax.ShapeDtypeStruct((DEPTH, n, w_mod.shape[2]), F32),
        compiler_params=_cparams(2),
        name="modulation",
    )(cvec, w_mod, b_mod.reshape(DEPTH, 1, -1))


def _mod_spec(per_batch, n_axes):
    if n_axes == 1:
        return pl.BlockSpec((1, 8, D_MODEL), (lambda b: (b, 0, 0)) if per_batch else (lambda b: (0, 0, 0)))
    return pl.BlockSpec((1, 8, D_MODEL), (lambda b, i: (b, 0, 0)) if per_batch else (lambda b, i: (0, 0, 0)))


def _inproj_kernel(x_ref, mod_ref, w_ref, o_ref):
    h = x_ref[0] * (1.0 + mod_ref[0, 1:2, :]) + mod_ref[0, 0:1, :]
    o_ref[0] = jnp.dot(h.astype(BF16), w_ref[...], preferred_element_type=F32)


def _inproj(x, mod, w_bf16, per_batch):
    B, L, D = x.shape
    N = w_bf16.shape[1]
    tm = 256
    return pl.pallas_call(
        _inproj_kernel,
        grid=(B, L // tm),
        in_specs=[pl.BlockSpec((1, tm, D), lambda b, i: (b, i, 0)),
                  _mod_spec(per_batch, 2),
                  pl.BlockSpec((D, N), lambda b, i: (0, 0))],
        out_specs=pl.BlockSpec((1, tm, N), lambda b, i: (b, i, 0)),
        out_shape=jax.ShapeDtypeStruct((B, L, N), F32),
        compiler_params=_cparams(2),
        name="inproj",
    )(x, mod, w_bf16)


HGRN_HEADS_PER_STEP = 2


def _hgrn_block(d, z, q, v, lb, st, sums_b, tri, rowchunk):
    blk = LANES
    nch = blk // HGRN_CHUNK
    logf = jnp.log(lb + (1.0 - lb) * jax.nn.sigmoid(z))
    key = (1.0 - lb) * jax.nn.sigmoid(-z)
    sums = jnp.dot(sums_b, jnp.concatenate(_split3(logf), axis=1), preferred_element_type=F32)
    sums = sums[:, 0:A_DK] + sums[:, A_DK:2 * A_DK] + sums[:, 2 * A_DK:3 * A_DK]
    b, tot = sums[0:blk], sums[blk:2 * blk]
    qd = (q * jnp.exp(b)).astype(BF16)
    kd = (key * jnp.exp(-b)).astype(BF16)
    ke = key * jnp.exp(tot - b)
    scores = jnp.where(tri, _dot_nt(qd, kd), 0.0)
    o = _dot(scores, v)
    ke_exp = jnp.concatenate([jnp.where(rowchunk == cc, ke, 0.0).astype(BF16) for cc in range(nch)], axis=1)
    ut = jnp.dot(v.T.astype(BF16), ke_exp, preferred_element_type=F32)
    dec = jnp.exp(tot)
    outs = [None] * nch
    for cc in (range(nch) if d == 0 else reversed(range(nch))):
        lo = cc * HGRN_CHUNK
        outs[cc] = _dot_nt(qd[lo:lo + HGRN_CHUNK], st)
        st = st * dec[lo:lo + 1, :] + ut[:, cc * A_DK:(cc + 1) * A_DK]
    return o + jnp.concatenate(outs, axis=0), st


def _hgrn_kernel(q_ref, zf_ref, zb_ref, v_ref, g_ref, lb_ref, gn_ref, s0_ref, o_ref, sfin_ref, obw_scr, *, nblk):
    blk = LANES
    hps = HGRN_HEADS_PER_STEP
    gn = gn_ref[...]
    r = lax.broadcasted_iota(I32, (blk, blk), 0)
    c = lax.broadcasted_iota(I32, (blk, blk), 1)
    same = (r // HGRN_CHUNK) == (c // HGRN_CHUNK)
    rowchunk = r // HGRN_CHUNK
    ones_f = jnp.where(same, 1.0, 0.0)
    tri = [jnp.logical_and(same, c <= r), jnp.logical_and(same, c >= r)]
    sums_b = [jnp.concatenate([jnp.where(t, 1.0, 0.0), ones_f], axis=0).astype(BF16) for t in tri]
    z_refs = (zf_ref, zb_ref)

    def body(i, sts):
        new = []
        for hh in range(hps):
            cols = slice(hh * LANES, (hh + 1) * LANES)
            lb = lb_ref[hh]
            for d in (0, 1):
                bi = i if d == 0 else nblk - 1 - i
                rows = pl.ds(pl.multiple_of(bi * blk, blk), blk)
                o, st = _hgrn_block(d, z_refs[d][0, rows, cols], q_ref[0, rows, cols], v_ref[0, rows, cols], lb,
                                    sts[hh * 2 + d], sums_b[d], tri[d], rowchunk)
                if d == 0:
                    o_ref[0, rows, cols] = o
                else:
                    obw_scr[rows, cols] = o
                new.append(st)
        return tuple(new)

    init = tuple(s0_ref[0, d, hh].T for hh in range(hps) for d in (0, 1))
    sts = lax.fori_loop(0, nblk, body, init)
    for hh in range(hps):
        for d in (0, 1):
            sfin_ref[0, d, hh] = sts[hh * 2 + d].T

    def finish(i, carry):
        rows = pl.ds(pl.multiple_of(i * blk, blk), blk)
        for hh in range(hps):
            cols = slice(hh * LANES, (hh + 1) * LANES)
            t = o_ref[0, rows, cols] + obw_scr[rows, cols]
            ms = jnp.mean(jnp.square(t), axis=-1, keepdims=True)
            o_ref[0, rows, cols] = t * lax.rsqrt(ms + LN_EPS) * gn * jax.nn.silu(g_ref[0, rows, cols])
        return carry

    lax.fori_loop(0, nblk, finish, 0)


def _hgrn(p, lb, gn, s0):
    B, L, _ = p.shape
    H = A_HEADS
    hps = HGRN_HEADS_PER_STEP
    w = hps * LANES

    def col(off):
        return pl.BlockSpec((1, L, w), lambda b, h, off=off: (b, 0, off + h))

    st_spec = pl.BlockSpec((1, 2, hps, A_DK, A_DV), lambda b, h: (b, 0, h, 0, 0))
    ng = H // hps
    return pl.pallas_call(
        functools.partial(_hgrn_kernel, nblk=L // LANES),
        grid=(B, ng),
        in_specs=[col(0), col(ng), col(2 * ng), col(3 * ng), col(4 * ng),
                  pl.BlockSpec((hps, 1, A_DK), lambda b, h: (h, 0, 0)),
                  pl.BlockSpec((1, A_DV), lambda b, h: (0, 0)),
                  st_spec],
        out_specs=[pl.BlockSpec((1, L, w), lambda b, h: (b, 0, h)), st_spec],
        out_shape=[jax.ShapeDtypeStruct((B, L, H * A_DV), F32), jax.ShapeDtypeStruct((B, 2, H, A_DK, A_DV), F32)],
        scratch_shapes=[pltpu.VMEM((L, w), F32)],
        compiler_params=_cparams(2),
        name="hgrn",
    )(p, p, p, p, p, lb.reshape(H, 1, A_DK), gn.reshape(1, A_DV), s0)


def _rope_tables(L, width):
    quarter = C_DH // 4
    t = jnp.arange(L)
    rows = (t // GRID_W).astype(F32)
    cols = (t % GRID_W).astype(F32)
    inv = ROPE_BASE ** (-jnp.arange(quarter, dtype=F32) / quarter)
    ang_r = rows[:, None] * inv[None, :]
    ang_c = cols[:, None] * inv[None, :]
    cos = jnp.concatenate([jnp.cos(ang_r), jnp.cos(ang_r), jnp.cos(ang_c), jnp.cos(ang_c)], -1)
    sin = jnp.concatenate([-jnp.sin(ang_r), jnp.sin(ang_r), -jnp.sin(ang_c), jnp.sin(ang_c)], -1)
    reps = width // C_DH
    return jnp.tile(cos, (1, reps)), jnp.tile(sin, (1, reps))


def _rope(x, cos, sin):
    n = x.shape[-1]
    q = C_DH // 4
    lane = lax.broadcasted_iota(I32, x.shape, x.ndim - 1)
    partner = jnp.where((lane & q) == 0, pltpu.roll(x, n - q, axis=x.ndim - 1), pltpu.roll(x, q, axis=x.ndim - 1))
    return x * cos + partner * sin


def _diff_kernel(*refs, L, n_ctx, lam_init, tq):
    if n_ctx:
        q_ref, k_ref, v_ref, lam_ref, gn_ref, kc_ref, vc_ref, cos_ref, sin_ref, o_ref, kk_scr, vt_scr, s_scr = refs
    else:
        q_ref, k_ref, v_ref, lam_ref, gn_ref, o_ref, kk_scr, vt_scr, s_scr = refs
    lam = lam_ref[...]
    lam_full = (jnp.exp(jnp.sum(lam[0:1] * lam[1:2], keepdims=True)) - jnp.exp(jnp.sum(lam[2:3] * lam[3:4], keepdims=True))
                + lam_init)
    gn = gn_ref[...]
    lane_k = lax.broadcasted_iota(I32, (L, LANES), 1)
    k_lat = k_ref[0]
    if n_ctx:
        k_lat = _rope(k_lat, cos_ref[...], sin_ref[...])
        lane_c = lax.broadcasted_iota(I32, (n_ctx, LANES), 1)
        k_ctx = kc_ref[0, 0, 0]
        for i in (0, 1):
            kk_scr[i, 0:n_ctx, :] = jnp.where((lane_c >= B_DH) if i else (lane_c < B_DH), k_ctx, 0.0).astype(BF16)
        vt_scr[:, 0:n_ctx] = vc_ref[0, 0, 0].T.astype(BF16)
    for i in (0, 1):
        kk_scr[i, n_ctx:, :] = jnp.where((lane_k >= B_DH) if i else (lane_k < B_DH), k_lat, 0.0).astype(BF16)
    vt_scr[:, n_ctx:] = v_ref[0].T.astype(BF16)
    qscale = B_DH ** -0.5 * LOG2E

    def scores(j, slot):
        rows = slice(j * tq, (j + 1) * tq)
        q = q_ref[0, rows, :]
        if n_ctx:
            q = _rope(q, cos_ref[rows, :], sin_ref[rows, :])
        qb = (q * qscale).astype(BF16)
        for i in (0, 1):
            s_scr[slot, i] = lax.dot_general(kk_scr[i], qb, (((1,), (1,)), ((), ())), preferred_element_type=F32)

    def attend(j, slot):
        vt = vt_scr[...]
        outs = []
        for i in (0, 1):
            s = s_scr[slot, i]
            m = jnp.max(s, axis=0, keepdims=True)
            e = jnp.exp2(s - m)
            den = jnp.sum(e, axis=0, keepdims=True)
            outs.append(jnp.dot(vt, e.astype(BF16), preferred_element_type=F32) / den)
        o = outs[0] - lam_full * outs[1]
        ms = jnp.mean(jnp.square(o), axis=0, keepdims=True)
        o_ref[0, j * tq:(j + 1) * tq, :] = (o * lax.rsqrt(ms + LN_EPS)).T * gn * (1.0 - lam_init)

    nq = L // tq
    scores(0, 0)
    for j in range(nq):
        if j + 1 < nq:
            scores(j + 1, (j + 1) % 2)
        attend(j, j % 2)


def _diff_attn(p, lam, gn, lam_init, ctx=None):
    B, L, _ = p.shape
    H = B_HEADS
    off = 5 * A_HEADS

    def col(o):
        return pl.BlockSpec((1, L, LANES), lambda b, h, o=o: (b, 0, o + h))

    in_specs = [col(off), col(off + H), col(off + 2 * H),
                pl.BlockSpec(lam.shape, lambda b, h: (0, 0)),
                pl.BlockSpec((1, LANES), lambda b, h: (0, 0))]
    args = [p, p, p, lam, gn.reshape(1, LANES)]
    n_ctx = 0
    if ctx is not None:
        kc, vc, j = ctx
        n_ctx = kc.shape[3]
        cos, sin = _rope_tables(L, LANES)
        in_specs += [pl.BlockSpec((1, 1, 1, n_ctx, LANES), lambda b, h, j=j: (b, j, h, 0, 0)),
                     pl.BlockSpec((1, 1, 1, n_ctx, LANES), lambda b, h, j=j: (b, j, h, 0, 0)),
                     pl.BlockSpec((L, LANES), lambda b, h: (0, 0)),
                     pl.BlockSpec((L, LANES), lambda b, h: (0, 0))]
        args += [kc, vc, cos, sin]
    tq = min(L, 256)
    return pl.pallas_call(
        functools.partial(_diff_kernel, L=L, n_ctx=n_ctx, lam_init=lam_init, tq=tq),
        grid=(B, H),
        in_specs=in_specs,
        out_specs=pl.BlockSpec((1, L, LANES), lambda b, h: (b, 0, h)),
        out_shape=jax.ShapeDtypeStruct((B, L, H * LANES), F32),
        scratch_shapes=[pltpu.VMEM((2, n_ctx + L, LANES), BF16), pltpu.VMEM((LANES, n_ctx + L), BF16),
                        pltpu.VMEM((2, 2, n_ctx + L, tq), F32)],
        compiler_params=_cparams(2),
        name="diff_attn",
    )(*args)


def _gqa_place_keys(k2, kp_scr):
    lane = lax.broadcasted_iota(I32, k2.shape, 1)
    swapped = pltpu.roll(k2, C_DH, axis=1)
    for kh in (0, 1):
        for u in (0, 1):
            src = k2 if kh == u else swapped
            kp_scr[kh * 2 + u] = jnp.where((lane >= C_DH) if u else (lane < C_DH), src, 0.0).astype(BF16)


def _sink_row(sink_ref, h0, h1, n):
    lane = lax.broadcasted_iota(I32, (1, 2 * n), 1)
    return jnp.where(lane < n, sink_ref[h0], sink_ref[h1]) * LOG2E


def _q_pair_rows(qb, kh):
    c0 = kh * C_GROUP * C_DH
    return jnp.concatenate([qb[:, c0:c0 + LANES], qb[:, c0 + LANES:c0 + 2 * LANES]], axis=0)


def _gqa_full_kernel(sink_ref, q_ref, k_ref, v_ref, o_ref, kp_scr, *, L):
    hp = pl.program_id(1)
    _gqa_place_keys(k_ref[0], kp_scr)
    vt = v_ref[0].T.astype(BF16)
    qb = (q_ref[0] * (C_DH ** -0.5 * LOG2E)).astype(BF16)
    pieces = []
    for kh in (0, 1):
        h0 = (hp * 2 + kh) * C_GROUP
        rhs = _q_pair_rows(qb, kh)
        outs = [None] * C_GROUP
        for u in (0, 1):
            s = lax.dot_general(kp_scr[kh * 2 + u], rhs, (((1,), (1,)), ((), ())), preferred_element_type=F32)
            sink = _sink_row(sink_ref, h0 + u, h0 + 2 + u, L)
            m = jnp.maximum(jnp.max(s, axis=0, keepdims=True), sink)
            e = jnp.exp2(s - m)
            den = jnp.sum(e, axis=0, keepdims=True) + jnp.exp2(sink - m)
            ot = jnp.dot(vt[kh * C_DH:(kh + 1) * C_DH], e.astype(BF16), preferred_element_type=F32) / den
            outs[u], outs[2 + u] = ot[:, 0:L], ot[:, L:2 * L]
        pieces += outs
    o_ref[0] = jnp.concatenate(pieces, axis=0).T


def _gqa_win_kernel(sink_ref, q_ref, k_ref, v_ref, kc_ref, vc_ref, cosq_ref, sinq_ref, o_ref,
                    kp_scr, kcp_scr, vt_scr, vct_scr, sl_scr, sc_scr, *, L):
    hp = pl.program_id(1)
    tq = WINDOW
    nblk = L // tq
    qscale = C_DH ** -0.5 * LOG2E
    _gqa_place_keys(_rope(k_ref[0], cosq_ref[:, 0:LANES], sinq_ref[:, 0:LANES]), kp_scr)
    for kh in (0, 1):
        kc = kc_ref[0, 0, kh]
        zero = jnp.zeros_like(kc)
        kcp_scr[kh * 2] = jnp.concatenate([kc, zero], axis=1).astype(BF16)
        kcp_scr[kh * 2 + 1] = jnp.concatenate([zero, kc], axis=1).astype(BF16)
        vct_scr[kh] = vc_ref[0, 0, kh].T.astype(BF16)
    for j in range(nblk):
        vt_scr[j] = v_ref[0, j * tq:(j + 1) * tq, :].T.astype(BF16)

    k_in = lax.broadcasted_iota(I32, (tq, 2 * tq), 0)
    q_in = lax.broadcasted_iota(I32, (tq, 2 * tq), 1) % tq
    keep_prev = k_in >= q_in
    keep_next = k_in <= q_in

    def window(n):
        return [b for b in (n - 1, n, n + 1) if 0 <= b < nblk]

    def scores(n, slot):
        rows = slice(n * tq, (n + 1) * tq)
        qb = (_rope(q_ref[0, rows, :], cosq_ref[rows, :], sinq_ref[rows, :]) * qscale).astype(BF16)
        blocks = window(n)
        k0, nk = blocks[0] * tq, len(blocks) * tq
        for kh in (0, 1):
            rhs = _q_pair_rows(qb, kh)
            for u in (0, 1):
                sl_scr[slot, kh * 2 + u, 0:nk, :] = lax.dot_general(kp_scr[kh * 2 + u, k0:k0 + nk, :], rhs,
                                                                    (((1,), (1,)), ((), ())), preferred_element_type=F32)
                sc_scr[slot, kh * 2 + u] = lax.dot_general(kcp_scr[kh * 2 + u], rhs, (((1,), (1,)), ((), ())),
                                                           preferred_element_type=F32)

    def attend(n, slot):
        blocks = window(n)
        pieces = []
        for kh in (0, 1):
            h0 = (hp * 2 + kh) * C_GROUP
            vt_loc = jnp.concatenate([vt_scr[b, kh * C_DH:(kh + 1) * C_DH, :] for b in blocks], axis=1)
            outs = [None] * C_GROUP
            for u in (0, 1):
                parts = []
                for t, b in enumerate(blocks):
                    s = sl_scr[slot, kh * 2 + u, t * tq:(t + 1) * tq, :]
                    if b != n:
                        s = jnp.where(keep_prev if b < n else keep_next, s, NEG_INF)
                    parts.append(s)
                s_loc = jnp.concatenate(parts, axis=0)
                s_ctx = sc_scr[slot, kh * 2 + u]
                sink = _sink_row(sink_ref, h0 + u, h0 + 2 + u, tq)
                m = jnp.maximum(jnp.maximum(jnp.max(s_loc, axis=0, keepdims=True), jnp.max(s_ctx, axis=0, keepdims=True)), sink)
                e_loc = jnp.exp2(s_loc - m)
                e_ctx = jnp.exp2(s_ctx - m)
                den = jnp.sum(e_loc, axis=0, keepdims=True) + jnp.sum(e_ctx, axis=0, keepdims=True) + jnp.exp2(sink - m)
                ot = (jnp.dot(vt_loc, e_loc.astype(BF16), preferred_element_type=F32)
                      + jnp.dot(vct_scr[kh], e_ctx.astype(BF16), preferred_element_type=F32)) / den
                outs[u], outs[2 + u] = ot[:, 0:tq], ot[:, tq:2 * tq]
            pieces += outs
        o_ref[0, n * tq:(n + 1) * tq, :] = jnp.concatenate(pieces, axis=0).T

    scores(0, 0)
    for n in range(nblk):
        if n + 1 < nblk:
            scores(n + 1, (n + 1) % 2)
        attend(n, n % 2)


def _gqa(p, sink, ctx=None):
    B, L, _ = p.shape
    qw = 2 * C_GROUP * C_DH
    nq = C_HEADS * C_DH // LANES
    nk = C_KV_HEADS * C_DH // LANES
    in_specs = [pl.BlockSpec(memory_space=pltpu.SMEM),
                pl.BlockSpec((1, L, qw), lambda b, h: (b, 0, h)),
                pl.BlockSpec((1, L, LANES), lambda b, h: (b, 0, nq + h)),
                pl.BlockSpec((1, L, LANES), lambda b, h: (b, 0, nq + nk + h))]
    args = [sink, p, p, p]
    scratch = [pltpu.VMEM((4, L, LANES), BF16)]
    if ctx is None:
        body = functools.partial(_gqa_full_kernel, L=L)
    else:
        kc, vc, j = ctx
        n_ctx = kc.shape[3]
        cos, sin = _rope_tables(L, qw)
        in_specs += [pl.BlockSpec((1, 1, 2, n_ctx, C_DH), lambda b, h, j=j: (b, j, h, 0, 0)),
                     pl.BlockSpec((1, 1, 2, n_ctx, C_DH), lambda b, h, j=j: (b, j, h, 0, 0)),
                     pl.BlockSpec((L, qw), lambda b, h: (0, 0)),
                     pl.BlockSpec((L, qw), lambda b, h: (0, 0))]
        args += [kc, vc, cos, sin]
        scratch += [pltpu.VMEM((4, n_ctx, LANES), BF16), pltpu.VMEM((L // WINDOW, LANES, WINDOW), BF16),
                    pltpu.VMEM((2, C_DH, n_ctx), BF16),
                    pltpu.VMEM((2, 4, 3 * WINDOW, 2 * WINDOW), F32), pltpu.VMEM((2, 4, n_ctx, 2 * WINDOW), F32)]
        body = functools.partial(_gqa_win_kernel, L=L)
    return pl.pallas_call(
        body,
        grid=(B, C_KV_HEADS // 2),
        in_specs=in_specs,
        out_specs=pl.BlockSpec((1, L, qw), lambda b, h: (b, 0, h)),
        out_shape=jax.ShapeDtypeStruct((B, L, C_HEADS * C_DH), F32),
        scratch_shapes=scratch,
        compiler_params=_cparams(2),
        name="gqa",
    )(*args)


def _layer_norm(t, g, b):
    mu = jnp.mean(t, axis=-1, keepdims=True)
    tc = t - mu
    var = jnp.mean(jnp.square(tc), axis=-1, keepdims=True)
    return tc * lax.rsqrt(var + LN_EPS) * g + b


def _outproj_kernel(*refs, n_in):
    y_ref = refs[0]
    a_refs = refs[1:1 + n_in]
    mod_ref, w_ref, lng_ref, lnb_ref, wr_ref, yo_ref, h2_ref, lg_ref = refs[1 + n_in:]
    mix = None
    k0 = 0
    for a_ref in a_refs:
        kw = a_ref.shape[2]
        part = jnp.dot(a_ref[0].astype(BF16), w_ref[k0:k0 + kw, :], preferred_element_type=F32)
        mix = part if mix is None else mix + part
        k0 += kw
    yn = _layer_norm(ALPHA * y_ref[0] + mod_ref[0, 2:3, :] * mix, lng_ref[...], lnb_ref[...])
    yo_ref[0] = yn
    h2 = yn * (1.0 + mod_ref[0, 4:5, :]) + mod_ref[0, 3:4, :]
    h_hi = h2.astype(BF16)
    h2_ref[0] = h_hi
    h_lo = (h2 - h_hi.astype(F32)).astype(BF16)
    both = jnp.dot(h_hi, wr_ref[...], preferred_element_type=F32)
    lg_ref[0] = (both[:, 0:LANES] + both[:, LANES:2 * LANES]
                 + jnp.dot(h_lo, wr_ref[:, 0:LANES], preferred_element_type=F32))


def _outproj(y, mixes, mod, w_bf16, lng, lnb, wr_pad, per_batch):
    B, L, D = y.shape
    tm = 256
    n_in = len(mixes)
    in_specs = [pl.BlockSpec((1, tm, D), lambda b, i: (b, i, 0))]
    in_specs += [pl.BlockSpec((1, tm, m.shape[2]), lambda b, i: (b, i, 0)) for m in mixes]
    in_specs += [_mod_spec(per_batch, 2),
                 pl.BlockSpec(w_bf16.shape, lambda b, i: (0, 0)),
                 pl.BlockSpec((1, D), lambda b, i: (0, 0)),
                 pl.BlockSpec((1, D), lambda b, i: (0, 0)),
                 pl.BlockSpec((D, 2 * LANES), lambda b, i: (0, 0))]
    return pl.pallas_call(
        functools.partial(_outproj_kernel, n_in=n_in),
        grid=(B, L // tm),
        in_specs=in_specs,
        out_specs=[pl.BlockSpec((1, tm, D), lambda b, i: (b, i, 0)),
                   pl.BlockSpec((1, tm, D), lambda b, i: (b, i, 0)),
                   pl.BlockSpec((1, tm, LANES), lambda b, i: (b, i, 0))],
        out_shape=[jax.ShapeDtypeStruct((B, L, D), F32), jax.ShapeDtypeStruct((B, L, D), BF16),
                   jax.ShapeDtypeStruct((B, L, LANES), F32)],
        compiler_params=_cparams(2),
        name="outproj",
    )(y, *mixes, mod, w_bf16, lng.reshape(1, D), lnb.reshape(1, D), wr_pad)


def _route_kernel(lg_ref, upper_ref, pos_ref, gs_ref, aff_scr, thr_scr, *, B, N, cap):
    E = N_EXPERTS

    def affinities(b, carry):
        lg = lg_ref[b]
        lane = lax.broadcasted_iota(I32, lg.shape, 1)
        lgm = jnp.where(lane < E, lg, -jnp.inf)
        m = jnp.max(lgm, axis=-1, keepdims=True)
        ex = jnp.exp(lgm - m)
        aff = ex / jnp.sum(ex, axis=-1, keepdims=True)
        aff_scr[pl.ds(pl.multiple_of(b * E, E), E), :] = aff.T[0:E, :]
        return carry

    lax.fori_loop(0, B, affinities, 0)
    aff_all = aff_scr[...]

    def bisect(_, carry):
        lo, hi = carry
        mid = lo + jnp.right_shift(hi - lo + 1, 1)
        cnt = jnp.sum((aff_all >= pltpu.bitcast(mid, F32)).astype(I32), axis=-1, keepdims=True)
        ok = cnt >= cap
        return jnp.where(ok, mid, lo), jnp.where(ok, hi, mid - 1)

    lo0 = jnp.zeros((B * E, 1), I32)
    hi0 = jnp.full((B * E, 1), 0x7F800000, I32)
    thr, _ = lax.fori_loop(0, 31, bisect, (lo0, hi0))
    thr_scr[...] = jnp.broadcast_to(thr, (B * E, LANES))

    def select(b, carry):
        rows = pl.ds(pl.multiple_of(b * E, E), E)
        aff_t = aff_scr[rows, :]
        thr_b = pltpu.bitcast(thr_scr[rows, 0:1], F32)
        gt = aff_t > thr_b
        eq = aff_t == thr_b
        need = cap - jnp.sum(gt.astype(I32), axis=-1, keepdims=True)
        upper = upper_ref[...]
        tie_rank = jnp.dot(jnp.where(eq, 1.0, 0.0).astype(BF16), upper, preferred_element_type=F32)
        sel = jnp.logical_or(gt, jnp.logical_and(eq, tie_rank < need.astype(F32)))
        slot = jnp.dot(jnp.where(sel, 1.0, 0.0).astype(BF16), upper, preferred_element_type=F32)
        pos = jnp.where(sel, slot.astype(I32), -1)
        pos_ref[b] = pos
        slot_id = lax.broadcasted_iota(I32, (cap, N), 0)
        for e in range(E):
            hit = slot_id == pos[e:e + 1, :]
            gcol = jnp.sum(jnp.where(hit, aff_t[e:e + 1, :], 0.0), axis=-1, keepdims=True)
            gs_ref[b, e] = jnp.broadcast_to(gcol, (cap, LANES))
        return carry

    lax.fori_loop(0, B, select, 0)


def _route(logits):
    B, N, _ = logits.shape
    E = N_EXPERTS
    cap = (CAP_FACTOR * N) // E
    upper = (jnp.arange(N)[:, None] < jnp.arange(N)[None, :]).astype(BF16)
    return pl.pallas_call(
        functools.partial(_route_kernel, B=B, N=N, cap=cap),
        grid=(1,),
        in_specs=[pl.BlockSpec((B, N, LANES), lambda i: (0, 0, 0)),
                  pl.BlockSpec((N, N), lambda i: (0, 0))],
        out_specs=[pl.BlockSpec((B, E, N), lambda i: (0, 0, 0)),
                   pl.BlockSpec((B, E, cap, LANES), lambda i: (0, 0, 0, 0))],
        out_shape=[jax.ShapeDtypeStruct((B, E, N), I32),
                   jax.ShapeDtypeStruct((B, E, cap, LANES), F32)],
        scratch_shapes=[pltpu.VMEM((B * E, N), F32), pltpu.VMEM((B * E, LANES), I32)],
        compiler_params=_cparams(1),
        name="route",
    )(logits, upper)


def _gather_kernel(pos_ref, h_ref, xs_ref, *, N, cap):
    h = h_ref[0]
    slot_id = lax.broadcasted_iota(I32, (cap, N), 0)
    for e in range(N_EXPERTS):
        onehot = jnp.where(slot_id == pos_ref[0, e:e + 1, :], 1.0, 0.0).astype(BF16)
        xs_ref[e] = jnp.dot(onehot, h, preferred_element_type=F32).astype(BF16)


def _gather(pos, h2):
    B, N, D = h2.shape
    cap = (CAP_FACTOR * N) // N_EXPERTS
    return pl.pallas_call(
        functools.partial(_gather_kernel, N=N, cap=cap),
        grid=(B,),
        in_specs=[pl.BlockSpec((1, N_EXPERTS, N), lambda b: (b, 0, 0)),
                  pl.BlockSpec((1, N, D), lambda b: (b, 0, 0))],
        out_specs=pl.BlockSpec((N_EXPERTS, cap, D), lambda b: (0, b, 0)),
        out_shape=jax.ShapeDtypeStruct((N_EXPERTS, B * cap, D), BF16),
        compiler_params=_cparams(1),
        name="gather",
    )(pos, h2)


def _ffn_kernel(xp_ref, xs_ref, gp_ref, gs_ref, wg_ref, wu_ref, wd_ref, op_ref, os_ref, *, tr):
    wg = wg_ref[0, 0].astype(BF16)
    wu = wu_ref[0, 0].astype(BF16)
    wd = wd_ref[0, 0].astype(BF16)
    for x_ref, g_ref, o_ref in ((xp_ref, gp_ref, op_ref), (xs_ref, gs_ref, os_ref)):
        rows = x_ref.shape[1]
        cap = g_ref.shape[2]
        for r0 in range(0, rows, tr):
            x = x_ref[0, r0:r0 + tr, :]
            hid = jax.nn.silu(jnp.dot(x, wg, preferred_element_type=F32)) * jnp.dot(x, wu, preferred_element_type=F32)
            out = jnp.dot(hid.astype(BF16), wd, preferred_element_type=F32)
            gate = g_ref[r0 // cap:(r0 + tr) // cap, 0].reshape(tr, LANES)[:, 0:1]
            o_ref[0, r0:r0 + tr, :] = (out * gate).astype(BF16)


def _ffn(xs_p, xs_s, gs_p, gs_s, wg, wu, wd, l):
    E, rp, D = xs_p.shape
    rs = xs_s.shape[1]
    FF = wg.shape[3]
    tr = 512

    def gspec(g):
        return pl.BlockSpec((g.shape[0], 1, g.shape[2], LANES), lambda e: (0, e, 0, 0))

    return pl.pallas_call(
        functools.partial(_ffn_kernel, tr=tr),
        grid=(E,),
        in_specs=[pl.BlockSpec((1, rp, D), lambda e: (e, 0, 0)),
                  pl.BlockSpec((1, rs, D), lambda e: (e, 0, 0)),
                  gspec(gs_p), gspec(gs_s),
                  pl.BlockSpec((1, 1, D, FF), lambda e, l=l: (l, e, 0, 0)),
                  pl.BlockSpec((1, 1, D, FF), lambda e, l=l: (l, e, 0, 0)),
                  pl.BlockSpec((1, 1, FF, D), lambda e, l=l: (l, e, 0, 0))],
        out_specs=[pl.BlockSpec((1, rp, D), lambda e: (e, 0, 0)),
                   pl.BlockSpec((1, rs, D), lambda e: (e, 0, 0))],
        out_shape=[jax.ShapeDtypeStruct((E, rp, D), BF16), jax.ShapeDtypeStruct((E, rs, D), BF16)],
        compiler_params=_cparams(1),
        name="ffn",
    )(xs_p, xs_s, gs_p, gs_s, wg, wu, wd)


def _combine_kernel(pos_ref, out_ref, y_ref, mod_ref, lng_ref, lnb_ref, yo_ref, *, N, cap):
    slot_id = lax.broadcasted_iota(I32, (cap, N), 0)
    onehot = jnp.concatenate(
        [jnp.where(slot_id == pos_ref[0, e:e + 1, :], 1.0, 0.0).astype(BF16) for e in range(N_EXPERTS)], axis=0)
    outs = out_ref[...].reshape(N_EXPERTS * cap, out_ref.shape[2])
    ff = lax.dot_general(onehot, outs, (((0,), (0,)), ((), ())), preferred_element_type=F32)
    yo_ref[0] = _layer_norm(ALPHA * y_ref[0] + mod_ref[0, 5:6, :] * ff, lng_ref[...], lnb_ref[...])


def _combine(pos, outs, y, mod, lng, lnb, per_batch):
    B, N, D = y.shape
    cap = (CAP_FACTOR * N) // N_EXPERTS
    return pl.pallas_call(
        functools.partial(_combine_kernel, N=N, cap=cap),
        grid=(B,),
        in_specs=[pl.BlockSpec((1, N_EXPERTS, N), lambda b: (b, 0, 0)),
                  pl.BlockSpec((N_EXPERTS, cap, D), lambda b: (0, b, 0)),
                  pl.BlockSpec((1, N, D), lambda b: (b, 0, 0)),
                  _mod_spec(per_batch, 1),
                  pl.BlockSpec((1, D), lambda b: (0, 0)),
                  pl.BlockSpec((1, D), lambda b: (0, 0))],
        out_specs=pl.BlockSpec((1, N, D), lambda b: (b, 0, 0)),
        out_shape=jax.ShapeDtypeStruct((B, N, D), F32),
        compiler_params=_cparams(1),
        name="combine",
    )(pos, outs, y, mod, lng.reshape(1, D), lnb.reshape(1, D))


def _to_heads(x, n_heads):
    B, L, _ = x.shape
    return x.reshape(B, L, n_heads, -1).transpose(0, 2, 1, 3)


def kernel(x_prompt, x_sample, state_hgrn, cache_diff_k, cache_diff_v, cache_win_k, cache_win_v, c, c_ctx, w_mod, b_mod, ln_g, ln_b, w_in_even, w_out_even, hgrn_lb_logits, hgrn_norm_g, diff_lambda, diff_norm_g, w_in_odd, w_out_odd, win_sink, w_router, w_exp_gate, w_exp_up, w_exp_down):
    yp, ys = x_prompt, x_sample
    nb_s = ys.shape[0]
    lb_all = jnp.cumsum(jax.nn.softmax(hgrn_lb_logits.astype(F32), axis=0), axis=0)
    cvec = jnp.zeros((16, D_MODEL), F32).at[:nb_s].set(c).at[nb_s].set(c_ctx)
    mods = _modulation(cvec, w_mod, b_mod).reshape(DEPTH, 16, 6, D_MODEL)
    mods = jnp.pad(mods, ((0, 0), (0, 0), (0, 2), (0, 0)))
    wr_f32 = jnp.pad(w_router, ((0, 0), (0, 0), (0, LANES - N_EXPERTS)))
    wr_hi = wr_f32.astype(BF16)
    wr_pad = jnp.concatenate([wr_hi, (wr_f32 - wr_hi.astype(F32)).astype(BF16)], axis=-1)
    new_state = new_dk = new_dv = new_wk = new_wv = None
    for l in range(DEPTH):
        mod_p = mods[l, nb_s:nb_s + 1]
        mod_s = mods[l, :nb_s]
        j = l // 2
        if l % 2 == 0:
            lam_init = 0.8 - 0.6 * math.exp(-0.3 * l)
            w_in = w_in_even[j].astype(BF16)
            w_out = w_out_even[j].astype(BF16)
            pp = _inproj(yp, mod_p, w_in, False)
            ps = _inproj(ys, mod_s, w_in, True)
            zero_state = jnp.zeros((yp.shape[0], 2, A_HEADS, A_DK, A_DV), F32)
            oa_p, st_p = _hgrn(pp, lb_all[l], hgrn_norm_g[j], zero_state)
            oa_s, _ = _hgrn(ps, lb_all[l], hgrn_norm_g[j], state_hgrn[:, j])
            ob_p = _diff_attn(pp, diff_lambda[j], diff_norm_g[j], lam_init)
            ob_s = _diff_attn(ps, diff_lambda[j], diff_norm_g[j], lam_init, ctx=(cache_diff_k, cache_diff_v, j))
            mix_p, mix_s = [oa_p, ob_p], [oa_s, ob_s]
            k0 = 3 * A_HEADS * A_DK + 2 * A_HEADS * A_DV + B_HEADS * 2 * B_DH
            new_state = st_p[:, None]
            new_dk = _to_heads(pp[..., k0:k0 + B_HEADS * 2 * B_DH], B_HEADS)[:, None]
            new_dv = _to_heads(pp[..., k0 + B_HEADS * 2 * B_DH:], B_HEADS)[:, None]
        else:
            w_in = w_in_odd[j].astype(BF16)
            w_out = w_out_odd[j].astype(BF16)
            pp = _inproj(yp, mod_p, w_in, False)
            ps = _inproj(ys, mod_s, w_in, True)
            mix_p = [_gqa(pp, win_sink[j])]
            mix_s = [_gqa(ps, win_sink[j], ctx=(cache_win_k, cache_win_v, j))]
            k0 = C_HEADS * C_DH
            new_wk = _to_heads(pp[..., k0:k0 + C_KV_HEADS * C_DH], C_KV_HEADS)[:, None]
            new_wv = _to_heads(pp[..., k0 + C_KV_HEADS * C_DH:], C_KV_HEADS)[:, None]
        yp, h2p, lgp = _outproj(yp, mix_p, mod_p, w_out, ln_g[l, 0], ln_b[l, 0], wr_pad[l], False)
        ys, h2s, lgs = _outproj(ys, mix_s, mod_s, w_out, ln_g[l, 0], ln_b[l, 0], wr_pad[l], True)
        pos_p, gs_p = _route(lgp)
        pos_s, gs_s = _route(lgs)
        xs_p = _gather(pos_p, h2p)
        xs_s = _gather(pos_s, h2s)
        out_p, out_s = _ffn(xs_p, xs_s, gs_p, gs_s, w_exp_gate, w_exp_up, w_exp_down, l)
        yp = _combine(pos_p, out_p, yp, mod_p, ln_g[l, 1], ln_b[l, 1], False)
        ys = _combine(pos_s, out_s, ys, mod_s, ln_g[l, 1], ln_b[l, 1], True)
    return (yp, ys, new_state, new_dk, new_dv, new_wk, new_wv)
```

```python
import functools
import math

import jax
import jax.numpy as jnp
from jax import lax
from jax.experimental import pallas as pl
from jax.experimental.pallas import tpu as pltpu

F32 = jnp.float32
BF16 = jnp.bfloat16
I32 = jnp.int32

D_MODEL = 1024
DEPTH = 2
GRID_W = 64
A_HEADS = 4
A_DK = 128
A_DV = 128
HGRN_CHUNK = 16
B_HEADS = 4
B_DH = 64
C_HEADS = 16
C_KV_HEADS = 4
C_GROUP = C_HEADS // C_KV_HEADS
C_DH = 64
WINDOW = 128
N_EXPERTS = 16
CAP_FACTOR = 2
ALPHA = (2 * DEPTH) ** 0.25
LN_EPS = 1e-5
ROPE_BASE = 10000.0
NEG_INF = -1e30
LOG2E = math.log2(math.e)
LANES = 128
VMEM_LIMIT = 56 * 1024 * 1024


def _cparams(n_axes):
    return pltpu.CompilerParams(dimension_semantics=("arbitrary",) * n_axes, vmem_limit_bytes=VMEM_LIMIT)


def _dot(a, b):
    return jnp.dot(a.astype(BF16), b.astype(BF16), preferred_element_type=F32)


def _dot_nt(a, b):
    return lax.dot_general(a.astype(BF16), b.astype(BF16), (((1,), (1,)), ((), ())), preferred_element_type=F32)


def _dot_tn(a, b):
    return lax.dot_general(a.astype(BF16), b.astype(BF16), (((0,), (0,)), ((), ())), preferred_element_type=F32)


def _split3(x):
    x1 = x.astype(BF16)
    r1 = x - x1.astype(F32)
    x2 = r1.astype(BF16)
    x3 = (r1 - x2.astype(F32)).astype(BF16)
    return x1, x2, x3


def _dot_sel(m, x):
    x1, x2, x3 = _split3(x)
    return (jnp.dot(m, x1, preferred_element_type=F32) + jnp.dot(m, x2, preferred_element_type=F32)
            + jnp.dot(m, x3, preferred_element_type=F32))


def _mod_kernel(c_ref, w_ref, b_ref, o_ref):
    a = jax.nn.silu(c_ref[...])
    w = w_ref[0]
    a_hi = a.astype(BF16)
    a_lo = (a - a_hi.astype(F32)).astype(BF16)
    w_hi = w.astype(BF16)
    w_lo = (w - w_hi.astype(F32)).astype(BF16)
    o_ref[0] = (jnp.dot(a_hi, w_hi, preferred_element_type=F32) + jnp.dot(a_hi, w_lo, preferred_element_type=F32)
                + jnp.dot(a_lo, w_hi, preferred_element_type=F32) + b_ref[0])


def _modulation(cvec, w_mod, b_mod):
    n, d = cvec.shape
    tn = 1024
    nt = w_mod.shape[2] // tn
    return pl.pallas_call(
        _mod_kernel,
        grid=(DEPTH, nt),
        in_specs=[pl.BlockSpec((n, d), lambda l, j: (0, 0)),
                  pl.BlockSpec((1, d, tn), lambda l, j: (l, 0, j)),
                  pl.BlockSpec((1, 1, tn), lambda l, j: (l, 0, j))],
        out_specs=pl.BlockSpec((1, n, tn), lambda l, j: (l, 0, j)),
        out_shape=jax.ShapeDtypeStruct((DEPTH, n, w_mod.shape[2]), F32),
        compiler_params=_cparams(2),
        name="modulation",
    )(cvec, w_mod, b_mod.reshape(DEPTH, 1, -1))


def _mod_spec(per_batch, n_axes):
    if n_axes == 1:
        return pl.BlockSpec((1, 8, D_MODEL), (lambda b: (b, 0, 0)) if per_batch else (lambda b: (0, 0, 0)))
    return pl.BlockSpec((1, 8, D_MODEL), (lambda b, i: (b, 0, 0)) if per_batch else (lambda b, i: (0, 0, 0)))


def _inproj_kernel(x_ref, mod_ref, w_ref, o_ref):
    h = x_ref[0] * (1.0 + mod_ref[0, 1:2, :]) + mod_ref[0, 0:1, :]
    o_ref[0] = jnp.dot(h.astype(BF16), w_ref[...], preferred_element_type=F32)


def _inproj(x, mod, w_bf16, per_batch):
    B, L, D = x.shape
    N = w_bf16.shape[1]
    tm = 256
    return pl.pallas_call(
        _inproj_kernel,
        grid=(B, L // tm),
        in_specs=[pl.BlockSpec((1, tm, D), lambda b, i: (b, i, 0)),
                  _mod_spec(per_batch, 2),
                  pl.BlockSpec((D, N), lambda b, i: (0, 0))],
        out_specs=pl.BlockSpec((1, tm, N), lambda b, i: (b, i, 0)),
        out_shape=jax.ShapeDtypeStruct((B, L, N), F32),
        compiler_params=_cparams(2),
        name="inproj",
    )(x, mod, w_bf16)


HGRN_HEADS_PER_STEP = 2


def _hgrn_block(d, z, q, v, lb, st, sums_b, tri, adj, first, even, rowpair):
    blk = LANES
    npair = blk // (2 * HGRN_CHUNK)
    logf = jnp.log(lb + (1.0 - lb) * jax.nn.sigmoid(z))
    key = (1.0 - lb) * jax.nn.sigmoid(-z)
    g_hi = logf.astype(BF16)
    g_lo = (logf - g_hi.astype(F32)).astype(BF16)
    sums = jnp.dot(sums_b, jnp.concatenate([g_hi, g_lo], axis=1), preferred_element_type=F32)
    sums = sums[:, 0:A_DK] + sums[:, A_DK:2 * A_DK]
    b, tot = sums[0:blk], sums[blk:2 * blk]
    qd_f = q * jnp.exp(b)
    qd = qd_f.astype(BF16)
    kd = (key * jnp.exp(-b)).astype(BF16)
    ke = key * jnp.exp(tot - b)
    dec = jnp.exp(tot)
    partner = jnp.where(even, jnp.concatenate([dec[HGRN_CHUNK:], dec[:HGRN_CHUNK]], axis=0),
                        jnp.concatenate([dec[-HGRN_CHUNK:], dec[:-HGRN_CHUNK]], axis=0))
    qx = jnp.where(first, qd_f, qd_f * partner).astype(BF16)
    ke2 = jnp.where(first, ke * partner, ke)
    scores = jnp.where(tri, _dot_nt(qd, kd), jnp.where(adj, _dot_nt(qd, ke), 0.0))
    o = _dot(scores, v)
    ke_exp = jnp.concatenate([jnp.where(rowpair == p, ke2, 0.0).astype(BF16) for p in range(npair)], axis=1)
    ut = jnp.dot(v.T.astype(BF16), ke_exp, preferred_element_type=F32)
    outs = [None] * npair
    for p in (range(npair) if d == 0 else reversed(range(npair))):
        lo = p * 2 * HGRN_CHUNK
        outs[p] = _dot_nt(qx[lo:lo + 2 * HGRN_CHUNK], st)
        st = st * (dec[lo:lo + 1, :] * dec[lo + HGRN_CHUNK:lo + HGRN_CHUNK + 1, :]) + ut[:, p * A_DK:(p + 1) * A_DK]
    return o + jnp.concatenate(outs, axis=0), st


def _hgrn_kernel(q_ref, zf_ref, zb_ref, v_ref, g_ref, lb_ref, gn_ref, s0_ref, o_ref, sfin_ref, ofw_scr, obw_scr, *, nblk):
    blk = LANES
    hps = HGRN_HEADS_PER_STEP
    gn = gn_ref[...]
    r = lax.broadcasted_iota(I32, (blk, blk), 0)
    c = lax.broadcasted_iota(I32, (blk, blk), 1)
    same = (r // HGRN_CHUNK) == (c // HGRN_CHUNK)
    rowpair = r // (2 * HGRN_CHUNK)
    row_even = (r // HGRN_CHUNK) % 2 == 0
    ones_f = jnp.where(same, 1.0, 0.0)
    tri = [jnp.logical_and(same, c <= r), jnp.logical_and(same, c >= r)]
    first = [row_even, jnp.logical_not(row_even)]
    other = jnp.logical_and(rowpair == c // (2 * HGRN_CHUNK), jnp.logical_not(same))
    adj = [jnp.logical_and(other, jnp.logical_not(f)) for f in first]
    sums_b = [jnp.concatenate([jnp.where(t, 1.0, 0.0), ones_f], axis=0).astype(BF16) for t in tri]
    z_refs = (zf_ref, zb_ref)

    def body(i, sts):
        new = []
        for hh in range(hps):
            cols = slice(hh * LANES, (hh + 1) * LANES)
            lb = lb_ref[hh]
            for d in (0, 1):
                bi = i if d == 0 else nblk - 1 - i
                rows = pl.ds(pl.multiple_of(bi * blk, blk), blk)
                o, st = _hgrn_block(d, z_refs[d][0, rows, cols], q_ref[0, rows, cols], v_ref[0, rows, cols], lb,
                                    sts[hh * 2 + d], sums_b[d], tri[d], adj[d], first[d], row_even, rowpair)
                if d == 0:
                    ofw_scr[rows, cols] = o
                else:
                    obw_scr[rows, cols] = o
                new.append(st)
        return tuple(new)

    init = tuple(s0_ref[0, d, hh].T for hh in range(hps) for d in (0, 1))
    sts = lax.fori_loop(0, nblk, body, init)
    for hh in range(hps):
        for d in (0, 1):
            sfin_ref[0, d, hh] = sts[hh * 2 + d].T

    def finish(i, carry):
        rows = pl.ds(pl.multiple_of(i * blk, blk), blk)
        for hh in range(hps):
            cols = slice(hh * LANES, (hh + 1) * LANES)
            t = ofw_scr[rows, cols] + obw_scr[rows, cols]
            ms = jnp.mean(jnp.square(t), axis=-1, keepdims=True)
            o_ref[0, rows, cols] = (t * lax.rsqrt(ms + LN_EPS) * gn * jax.nn.silu(g_ref[0, rows, cols])).astype(BF16)
        return carry

    lax.fori_loop(0, nblk, finish, 0)


def _hgrn(p, lb, gn, s0):
    B, L, _ = p.shape
    H = A_HEADS
    hps = HGRN_HEADS_PER_STEP
    w = hps * LANES

    def col(off):
        return pl.BlockSpec((1, L, w), lambda b, h, off=off: (b, 0, off + h))

    st_spec = pl.BlockSpec((1, 2, hps, A_DK, A_DV), lambda b, h: (b, 0, h, 0, 0))
    ng = H // hps
    return pl.pallas_call(
        functools.partial(_hgrn_kernel, nblk=L // LANES),
        grid=(B, ng),
        in_specs=[col(0), col(ng), col(2 * ng), col(3 * ng), col(4 * ng),
                  pl.BlockSpec((hps, 1, A_DK), lambda b, h: (h, 0, 0)),
                  pl.BlockSpec((1, A_DV), lambda b, h: (0, 0)),
                  st_spec],
        out_specs=[pl.BlockSpec((1, L, w), lambda b, h: (b, 0, h)), st_spec],
        out_shape=[jax.ShapeDtypeStruct((B, L, H * A_DV), BF16), jax.ShapeDtypeStruct((B, 2, H, A_DK, A_DV), F32)],
        scratch_shapes=[pltpu.VMEM((L, w), F32), pltpu.VMEM((L, w), F32)],
        compiler_params=_cparams(2),
        name="hgrn",
    )(p, p, p, p, p, lb.reshape(H, 1, A_DK), gn.reshape(1, A_DV), s0)


def _rope_tables(L, width):
    quarter = C_DH // 4
    t = jnp.arange(L)
    rows = (t // GRID_W).astype(F32)
    cols = (t % GRID_W).astype(F32)
    inv = ROPE_BASE ** (-jnp.arange(quarter, dtype=F32) / quarter)
    ang_r = rows[:, None] * inv[None, :]
    ang_c = cols[:, None] * inv[None, :]
    cos = jnp.concatenate([jnp.cos(ang_r), jnp.cos(ang_r), jnp.cos(ang_c), jnp.cos(ang_c)], -1)
    sin = jnp.concatenate([-jnp.sin(ang_r), jnp.sin(ang_r), -jnp.sin(ang_c), jnp.sin(ang_c)], -1)
    reps = width // C_DH
    return jnp.tile(cos, (1, reps)), jnp.tile(sin, (1, reps))


def _rope(x, cos, sin):
    n = x.shape[-1]
    q = C_DH // 4
    lane = lax.broadcasted_iota(I32, x.shape, x.ndim - 1)
    partner = jnp.where((lane & q) == 0, pltpu.roll(x, n - q, axis=x.ndim - 1), pltpu.roll(x, q, axis=x.ndim - 1))
    return x * cos + partner * sin


def _diff_kernel(*refs, L, n_ctx, lam_init, tq):
    if n_ctx:
        q_ref, k_ref, v_ref, lam_ref, gn_ref, kc_ref, vc_ref, cos_ref, sin_ref, o_ref, kk_scr, vt_scr, s_scr = refs
    else:
        q_ref, k_ref, v_ref, lam_ref, gn_ref, o_ref, kk_scr, vt_scr, s_scr = refs
    lam = lam_ref[...]
    lam_full = (jnp.exp(jnp.sum(lam[0:1] * lam[1:2], keepdims=True)) - jnp.exp(jnp.sum(lam[2:3] * lam[3:4], keepdims=True))
                + lam_init)
    gn = gn_ref[...]
    lane_k = lax.broadcasted_iota(I32, (L, LANES), 1)
    k_lat = k_ref[0]
    if n_ctx:
        k_lat = _rope(k_lat, cos_ref[...], sin_ref[...])
        lane_c = lax.broadcasted_iota(I32, (n_ctx, LANES), 1)
        k_ctx = kc_ref[0, 0, 0]
        for i in (0, 1):
            kk_scr[i, 0:n_ctx, :] = jnp.where((lane_c >= B_DH) if i else (lane_c < B_DH), k_ctx, 0.0).astype(BF16)
        vt_scr[:, 0:n_ctx] = vc_ref[0, 0, 0].T.astype(BF16)
    for i in (0, 1):
        kk_scr[i, n_ctx:, :] = jnp.where((lane_k >= B_DH) if i else (lane_k < B_DH), k_lat, 0.0).astype(BF16)
    vt_scr[:, n_ctx:] = v_ref[0].T.astype(BF16)
    qscale = B_DH ** -0.5 * LOG2E

    def scores(j, slot):
        rows = slice(j * tq, (j + 1) * tq)
        q = q_ref[0, rows, :]
        if n_ctx:
            q = _rope(q, cos_ref[rows, :], sin_ref[rows, :])
        qb = (q * qscale).astype(BF16)
        for i in (0, 1):
            s_scr[slot, i] = lax.dot_general(kk_scr[i], qb, (((1,), (1,)), ((), ())), preferred_element_type=F32)

    def attend(j, slot):
        vt = vt_scr[...]
        outs = []
        for i in (0, 1):
            s = s_scr[slot, i]
            m = jnp.max(s, axis=0, keepdims=True)
            e = jnp.exp2(s - m)
            den = jnp.sum(e, axis=0, keepdims=True)
            outs.append(jnp.dot(vt, e.astype(BF16), preferred_element_type=F32) / den)
        o = outs[0] - lam_full * outs[1]
        ms = jnp.mean(jnp.square(o), axis=0, keepdims=True)
        o_ref[0, j * tq:(j + 1) * tq, :] = ((o * lax.rsqrt(ms + LN_EPS)).T * gn * (1.0 - lam_init)).astype(BF16)

    nq = L // tq
    scores(0, 0)
    for j in range(nq):
        if j + 1 < nq:
            scores(j + 1, (j + 1) % 2)
        attend(j, j % 2)


def _diff_attn(p, lam, gn, lam_init, ctx=None):
    B, L, _ = p.shape
    H = B_HEADS
    off = 5 * A_HEADS

    def col(o):
        return pl.BlockSpec((1, L, LANES), lambda b, h, o=o: (b, 0, o + h))

    in_specs = [col(off), col(off + H), col(off + 2 * H),
                pl.BlockSpec(lam.shape, lambda b, h: (0, 0)),
                pl.BlockSpec((1, LANES), lambda b, h: (0, 0))]
    args = [p, p, p, lam, gn.reshape(1, LANES)]
    n_ctx = 0
    if ctx is not None:
        kc, vc, j = ctx
        n_ctx = kc.shape[3]
        cos, sin = _rope_tables(L, LANES)
        in_specs += [pl.BlockSpec((1, 1, 1, n_ctx, LANES), lambda b, h, j=j: (b, j, h, 0, 0)),
                     pl.BlockSpec((1, 1, 1, n_ctx, LANES), lambda b, h, j=j: (b, j, h, 0, 0)),
                     pl.BlockSpec((L, LANES), lambda b, h: (0, 0)),
                     pl.BlockSpec((L, LANES), lambda b, h: (0, 0))]
        args += [kc, vc, cos, sin]
    tq = min(L, 256)
    return pl.pallas_call(
        functools.partial(_diff_kernel, L=L, n_ctx=n_ctx, lam_init=lam_init, tq=tq),
        grid=(B, H),
        in_specs=in_specs,
        out_specs=pl.BlockSpec((1, L, LANES), lambda b, h: (b, 0, h)),
        out_shape=jax.ShapeDtypeStruct((B, L, H * LANES), BF16),
        scratch_shapes=[pltpu.VMEM((2, n_ctx + L, LANES), BF16), pltpu.VMEM((LANES, n_ctx + L), BF16),
                        pltpu.VMEM((2, 2, n_ctx + L, tq), F32)],
        compiler_params=_cparams(2),
        name="diff_attn",
    )(*args)


def _gqa_place_keys(k2, kp_scr):
    lane = lax.broadcasted_iota(I32, k2.shape, 1)
    swapped = pltpu.roll(k2, C_DH, axis=1)
    for kh in (0, 1):
        for u in (0, 1):
            src = k2 if kh == u else swapped
            kp_scr[kh * 2 + u] = jnp.where((lane >= C_DH) if u else (lane < C_DH), src, 0.0).astype(BF16)


def _sink_row(sink_ref, h0, h1, n):
    lane = lax.broadcasted_iota(I32, (1, 2 * n), 1)
    return jnp.where(lane < n, sink_ref[h0], sink_ref[h1]) * LOG2E


def _q_pair_rows(qb, kh):
    c0 = kh * C_GROUP * C_DH
    return jnp.concatenate([qb[:, c0:c0 + LANES], qb[:, c0 + LANES:c0 + 2 * LANES]], axis=0)


def _gqa_full_kernel(sink_ref, q_ref, k_ref, v_ref, o_ref, kp_scr, *, L):
    hp = pl.program_id(1)
    _gqa_place_keys(k_ref[0], kp_scr)
    vt = v_ref[0].T.astype(BF16)
    qb = (q_ref[0] * (C_DH ** -0.5 * LOG2E)).astype(BF16)
    pieces = []
    for kh in (0, 1):
        h0 = (hp * 2 + kh) * C_GROUP
        rhs = _q_pair_rows(qb, kh)
        outs = [None] * C_GROUP
        for u in (0, 1):
            s = lax.dot_general(kp_scr[kh * 2 + u], rhs, (((1,), (1,)), ((), ())), preferred_element_type=F32)
            sink = _sink_row(sink_ref, h0 + u, h0 + 2 + u, L)
            m = jnp.maximum(jnp.max(s, axis=0, keepdims=True), sink)
            e = jnp.exp2(s - m)
            den = jnp.sum(e, axis=0, keepdims=True) + jnp.exp2(sink - m)
            ot = jnp.dot(vt[kh * C_DH:(kh + 1) * C_DH], e.astype(BF16), preferred_element_type=F32) / den
            outs[u], outs[2 + u] = ot[:, 0:L], ot[:, L:2 * L]
        pieces += outs
    o_ref[0] = jnp.concatenate(pieces, axis=0).T.astype(BF16)


def _gqa_win_kernel(sink_ref, q_ref, k_ref, v_ref, kc_ref, vc_ref, cosq_ref, sinq_ref, o_ref,
                    kp_scr, kcp_scr, vt_scr, vct_scr, sl_scr, sc_scr, *, L):
    hp = pl.program_id(1)
    tq = WINDOW
    nblk = L // tq
    qscale = C_DH ** -0.5 * LOG2E
    _gqa_place_keys(_rope(k_ref[0], cosq_ref[:, 0:LANES], sinq_ref[:, 0:LANES]), kp_scr)
    for kh in (0, 1):
        kc = kc_ref[0, 0, kh]
        zero = jnp.zeros_like(kc)
        kcp_scr[kh * 2] = jnp.concatenate([kc, zero], axis=1).astype(BF16)
        kcp_scr[kh * 2 + 1] = jnp.concatenate([zero, kc], axis=1).astype(BF16)
        vct_scr[kh] = vc_ref[0, 0, kh].T.astype(BF16)
    for j in range(nblk):
        vt_scr[j] = v_ref[0, j * tq:(j + 1) * tq, :].T.astype(BF16)

    k_in = lax.broadcasted_iota(I32, (tq, 2 * tq), 0)
    q_in = lax.broadcasted_iota(I32, (tq, 2 * tq), 1) % tq
    keep_prev = k_in >= q_in
    keep_next = k_in <= q_in

    def window(n):
        return [b for b in (n - 1, n, n + 1) if 0 <= b < nblk]

    def scores(n, slot):
        rows = slice(n * tq, (n + 1) * tq)
        qb = (_rope(q_ref[0, rows, :], cosq_ref[rows, :], sinq_ref[rows, :]) * qscale).astype(BF16)
        blocks = window(n)
        k0, nk = blocks[0] * tq, len(blocks) * tq
        for kh in (0, 1):
            rhs = _q_pair_rows(qb, kh)
            for u in (0, 1):
                sl_scr[slot, kh * 2 + u, 0:nk, :] = lax.dot_general(kp_scr[kh * 2 + u, k0:k0 + nk, :], rhs,
                                                                    (((1,), (1,)), ((), ())), preferred_element_type=F32)
                sc_scr[slot, kh * 2 + u] = lax.dot_general(kcp_scr[kh * 2 + u], rhs, (((1,), (1,)), ((), ())),
                                                           preferred_element_type=F32)

    def attend(n, slot):
        blocks = window(n)
        pieces = []
        for kh in (0, 1):
            h0 = (hp * 2 + kh) * C_GROUP
            vt_loc = jnp.concatenate([vt_scr[b, kh * C_DH:(kh + 1) * C_DH, :] for b in blocks], axis=1)
            outs = [None] * C_GROUP
            for u in (0, 1):
                parts = []
                for t, b in enumerate(blocks):
                    s = sl_scr[slot, kh * 2 + u, t * tq:(t + 1) * tq, :]
                    if b != n:
                        s = jnp.where(keep_prev if b < n else keep_next, s, NEG_INF)
                    parts.append(s)
                s_loc = jnp.concatenate(parts, axis=0)
                s_ctx = sc_scr[slot, kh * 2 + u]
                sink = _sink_row(sink_ref, h0 + u, h0 + 2 + u, tq)
                m = jnp.maximum(jnp.maximum(jnp.max(s_loc, axis=0, keepdims=True), jnp.max(s_ctx, axis=0, keepdims=True)), sink)
                e_loc = jnp.exp2(s_loc - m)
                e_ctx = jnp.exp2(s_ctx - m)
                den = jnp.sum(e_loc, axis=0, keepdims=True) + jnp.sum(e_ctx, axis=0, keepdims=True) + jnp.exp2(sink - m)
                ot = (jnp.dot(vt_loc, e_loc.astype(BF16), preferred_element_type=F32)
                      + jnp.dot(vct_scr[kh], e_ctx.astype(BF16), preferred_element_type=F32)) / den
                outs[u], outs[2 + u] = ot[:, 0:tq], ot[:, tq:2 * tq]
            pieces += outs
        o_ref[0, n * tq:(n + 1) * tq, :] = jnp.concatenate(pieces, axis=0).T.astype(BF16)

    scores(0, 0)
    for n in range(nblk):
        if n + 1 < nblk:
            scores(n + 1, (n + 1) % 2)
        attend(n, n % 2)


def _gqa(p, sink, ctx=None):
    B, L, _ = p.shape
    qw = 2 * C_GROUP * C_DH
    nq = C_HEADS * C_DH // LANES
    nk = C_KV_HEADS * C_DH // LANES
    in_specs = [pl.BlockSpec(memory_space=pltpu.SMEM),
                pl.BlockSpec((1, L, qw), lambda b, h: (b, 0, h)),
                pl.BlockSpec((1, L, LANES), lambda b, h: (b, 0, nq + h)),
                pl.BlockSpec((1, L, LANES), lambda b, h: (b, 0, nq + nk + h))]
    args = [sink, p, p, p]
    scratch = [pltpu.VMEM((4, L, LANES), BF16)]
    if ctx is None:
        body = functools.partial(_gqa_full_kernel, L=L)
    else:
        kc, vc, j = ctx
        n_ctx = kc.shape[3]
        cos, sin = _rope_tables(L, qw)
        in_specs += [pl.BlockSpec((1, 1, 2, n_ctx, C_DH), lambda b, h, j=j: (b, j, h, 0, 0)),
                     pl.BlockSpec((1, 1, 2, n_ctx, C_DH), lambda b, h, j=j: (b, j, h, 0, 0)),
                     pl.BlockSpec((L, qw), lambda b, h: (0, 0)),
                     pl.BlockSpec((L, qw), lambda b, h: (0, 0))]
        args += [kc, vc, cos, sin]
        scratch += [pltpu.VMEM((4, n_ctx, LANES), BF16), pltpu.VMEM((L // WINDOW, LANES, WINDOW), BF16),
                    pltpu.VMEM((2, C_DH, n_ctx), BF16),
                    pltpu.VMEM((2, 4, 3 * WINDOW, 2 * WINDOW), F32), pltpu.VMEM((2, 4, n_ctx, 2 * WINDOW), F32)]
        body = functools.partial(_gqa_win_kernel, L=L)
    return pl.pallas_call(
        body,
        grid=(B, C_KV_HEADS // 2),
        in_specs=in_specs,
        out_specs=pl.BlockSpec((1, L, qw), lambda b, h: (b, 0, h)),
        out_shape=jax.ShapeDtypeStruct((B, L, C_HEADS * C_DH), BF16),
        scratch_shapes=scratch,
        compiler_params=_cparams(2),
        name="gqa",
    )(*args)


def _layer_norm(t, g, b):
    mu = jnp.mean(t, axis=-1, keepdims=True)
    tc = t - mu
    var = jnp.mean(jnp.square(tc), axis=-1, keepdims=True)
    return tc * lax.rsqrt(var + LN_EPS) * g + b


def _outproj_kernel(*refs, n_in):
    y_ref = refs[0]
    a_refs = refs[1:1 + n_in]
    mod_ref, w_ref, lng_ref, lnb_ref, wr_ref, yo_ref, h2_ref, lg_ref = refs[1 + n_in:]
    mix = None
    k0 = 0
    for a_ref in a_refs:
        kw = a_ref.shape[2]
        part = jnp.dot(a_ref[0].astype(BF16), w_ref[k0:k0 + kw, :], preferred_element_type=F32)
        mix = part if mix is None else mix + part
        k0 += kw
    yn = _layer_norm(ALPHA * y_ref[0] + mod_ref[0, 2:3, :] * mix, lng_ref[...], lnb_ref[...])
    yo_ref[0] = yn
    h2 = yn * (1.0 + mod_ref[0, 4:5, :]) + mod_ref[0, 3:4, :]
    h_hi = h2.astype(BF16)
    h2_ref[0] = h_hi
    h_lo = (h2 - h_hi.astype(F32)).astype(BF16)
    both = jnp.dot(h_hi, wr_ref[...], preferred_element_type=F32)
    lg_ref[0] = (both[:, 0:LANES] + both[:, LANES:2 * LANES]
                 + jnp.dot(h_lo, wr_ref[:, 0:LANES], preferred_element_type=F32))


def _outproj(y, mixes, mod, w_bf16, lng, lnb, wr_pad, per_batch):
    B, L, D = y.shape
    tm = 256
    n_in = len(mixes)
    in_specs = [pl.BlockSpec((1, tm, D), lambda b, i: (b, i, 0))]
    in_specs += [pl.BlockSpec((1, tm, m.shape[2]), lambda b, i: (b, i, 0)) for m in mixes]
    in_specs += [_mod_spec(per_batch, 2),
                 pl.BlockSpec(w_bf16.shape, lambda b, i: (0, 0)),
                 pl.BlockSpec((1, D), lambda b, i: (0, 0)),
                 pl.BlockSpec((1, D), lambda b, i: (0, 0)),
                 pl.BlockSpec((D, 2 * LANES), lambda b, i: (0, 0))]
    return pl.pallas_call(
        functools.partial(_outproj_kernel, n_in=n_in),
        grid=(B, L // tm),
        in_specs=in_specs,
        out_specs=[pl.BlockSpec((1, tm, D), lambda b, i: (b, i, 0)),
                   pl.BlockSpec((1, tm, D), lambda b, i: (b, i, 0)),
                   pl.BlockSpec((1, tm, LANES), lambda b, i: (b, i, 0))],
        out_shape=[jax.ShapeDtypeStruct((B, L, D), F32), jax.ShapeDtypeStruct((B, L, D), BF16),
                   jax.ShapeDtypeStruct((B, L, LANES), F32)],
        compiler_params=_cparams(2),
        name="outproj",
    )(y, *mixes, mod, w_bf16, lng.reshape(1, D), lnb.reshape(1, D), wr_pad)


def _route_kernel(lg_ref, upper_ref, pos_ref, gs_ref, aff_scr, thr_scr, *, B, N, cap):
    E = N_EXPERTS

    def affinities(b, carry):
        lg = lg_ref[b]
        lane = lax.broadcasted_iota(I32, lg.shape, 1)
        lgm = jnp.where(lane < E, lg, -jnp.inf)
        m = jnp.max(lgm, axis=-1, keepdims=True)
        ex = jnp.exp(lgm - m)
        aff = ex / jnp.sum(ex, axis=-1, keepdims=True)
        aff_scr[pl.ds(pl.multiple_of(b * E, E), E), :] = aff.T[0:E, :]
        return carry

    lax.fori_loop(0, B, affinities, 0)
    aff_all = aff_scr[...]

    def bisect(_, carry):
        lo, hi = carry
        mid = lo + jnp.right_shift(hi - lo + 1, 1)
        cnt = jnp.sum((aff_all >= pltpu.bitcast(mid, F32)).astype(I32), axis=-1, keepdims=True)
        ok = cnt >= cap
        return jnp.where(ok, mid, lo), jnp.where(ok, hi, mid - 1)

    lo0 = jnp.zeros((B * E, 1), I32)
    hi0 = jnp.full((B * E, 1), 0x7F800000, I32)
    thr, _ = lax.fori_loop(0, 31, bisect, (lo0, hi0))
    thr_scr[...] = jnp.broadcast_to(thr, (B * E, LANES))

    def select(b, carry):
        rows = pl.ds(pl.multiple_of(b * E, E), E)
        aff_t = aff_scr[rows, :]
        thr_b = pltpu.bitcast(thr_scr[rows, 0:1], F32)
        gt = aff_t > thr_b
        eq = aff_t == thr_b
        need = cap - jnp.sum(gt.astype(I32), axis=-1, keepdims=True)
        upper = upper_ref[...]
        tie_rank = jnp.dot(jnp.where(eq, 1.0, 0.0).astype(BF16), upper, preferred_element_type=F32)
        sel = jnp.logical_or(gt, jnp.logical_and(eq, tie_rank < need.astype(F32)))
        slot = jnp.dot(jnp.where(sel, 1.0, 0.0).astype(BF16), upper, preferred_element_type=F32)
        pos = jnp.where(sel, slot.astype(I32), -1)
        pos_ref[b] = pos
        slot_id = lax.broadcasted_iota(I32, (cap, N), 0)
        for e in range(E):
            hit = slot_id == pos[e:e + 1, :]
            gcol = jnp.sum(jnp.where(hit, aff_t[e:e + 1, :], 0.0), axis=-1, keepdims=True)
            gs_ref[b, e] = jnp.broadcast_to(gcol, (cap, LANES))
        return carry

    lax.fori_loop(0, B, select, 0)


def _route(logits):
    B, N, _ = logits.shape
    E = N_EXPERTS
    cap = (CAP_FACTOR * N) // E
    upper = (jnp.arange(N)[:, None] < jnp.arange(N)[None, :]).astype(BF16)
    return pl.pallas_call(
        functools.partial(_route_kernel, B=B, N=N, cap=cap),
        grid=(1,),
        in_specs=[pl.BlockSpec((B, N, LANES), lambda i: (0, 0, 0)),
                  pl.BlockSpec((N, N), lambda i: (0, 0))],
        out_specs=[pl.BlockSpec((B, E, N), lambda i: (0, 0, 0)),
                   pl.BlockSpec((B, E, cap, LANES), lambda i: (0, 0, 0, 0))],
        out_shape=[jax.ShapeDtypeStruct((B, E, N), I32),
                   jax.ShapeDtypeStruct((B, E, cap, LANES), F32)],
        scratch_shapes=[pltpu.VMEM((B * E, N), F32), pltpu.VMEM((B * E, LANES), I32)],
        compiler_params=_cparams(1),
        name="route",
    )(logits, upper)


def _gather_kernel(pos_ref, h_ref, xs_ref, *, N, cap):
    slot_id = lax.broadcasted_iota(I32, (cap, N), 0)
    onehot = jnp.concatenate(
        [jnp.where(slot_id == pos_ref[0, e:e + 1, :], 1.0, 0.0).astype(BF16) for e in range(N_EXPERTS)], axis=0)
    xs = jnp.dot(onehot, h_ref[0], preferred_element_type=F32).astype(BF16)
    xs_ref[...] = xs.reshape(N_EXPERTS, cap, xs.shape[1])


def _gather(pos, h2):
    B, N, D = h2.shape
    cap = (CAP_FACTOR * N) // N_EXPERTS
    return pl.pallas_call(
        functools.partial(_gather_kernel, N=N, cap=cap),
        grid=(B,),
        in_specs=[pl.BlockSpec((1, N_EXPERTS, N), lambda b: (b, 0, 0)),
                  pl.BlockSpec((1, N, D), lambda b: (b, 0, 0))],
        out_specs=pl.BlockSpec((N_EXPERTS, cap, D), lambda b: (0, b, 0)),
        out_shape=jax.ShapeDtypeStruct((N_EXPERTS, B * cap, D), BF16),
        compiler_params=_cparams(1),
        name="gather",
    )(pos, h2)


def _ffn_kernel(xp_ref, xs_ref, gp_ref, gs_ref, wg_ref, wu_ref, wd_ref, op_ref, os_ref, *, tr):
    wg = wg_ref[0, 0].astype(BF16)
    wu = wu_ref[0, 0].astype(BF16)
    wd = wd_ref[0, 0].astype(BF16)
    for x_ref, g_ref, o_ref in ((xp_ref, gp_ref, op_ref), (xs_ref, gs_ref, os_ref)):
        rows = x_ref.shape[1]
        cap = g_ref.shape[2]
        for r0 in range(0, rows, tr):
            x = x_ref[0, r0:r0 + tr, :]
            hid = jax.nn.silu(jnp.dot(x, wg, preferred_element_type=F32)) * jnp.dot(x, wu, preferred_element_type=F32)
            out = jnp.dot(hid.astype(BF16), wd, preferred_element_type=F32)
            gate = g_ref[r0 // cap:(r0 + tr) // cap, 0].reshape(tr, LANES)[:, 0:1]
            o_ref[0, r0:r0 + tr, :] = (out * gate).astype(BF16)


def _ffn(xs_p, xs_s, gs_p, gs_s, wg, wu, wd, l):
    E, rp, D = xs_p.shape
    rs = xs_s.shape[1]
    FF = wg.shape[3]
    tr = 512

    def gspec(g):
        return pl.BlockSpec((g.shape[0], 1, g.shape[2], LANES), lambda e: (0, e, 0, 0))

    return pl.pallas_call(
        functools.partial(_ffn_kernel, tr=tr),
        grid=(E,),
        in_specs=[pl.BlockSpec((1, rp, D), lambda e: (e, 0, 0)),
                  pl.BlockSpec((1, rs, D), lambda e: (e, 0, 0)),
                  gspec(gs_p), gspec(gs_s),
                  pl.BlockSpec((1, 1, D, FF), lambda e, l=l: (l, e, 0, 0)),
                  pl.BlockSpec((1, 1, D, FF), lambda e, l=l: (l, e, 0, 0)),
                  pl.BlockSpec((1, 1, FF, D), lambda e, l=l: (l, e, 0, 0))],
        out_specs=[pl.BlockSpec((1, rp, D), lambda e: (e, 0, 0)),
                   pl.BlockSpec((1, rs, D), lambda e: (e, 0, 0))],
        out_shape=[jax.ShapeDtypeStruct((E, rp, D), BF16), jax.ShapeDtypeStruct((E, rs, D), BF16)],
        compiler_params=_cparams(1),
        name="ffn",
    )(xs_p, xs_s, gs_p, gs_s, wg, wu, wd)


def _combine_kernel(pos_ref, out_ref, y_ref, mod_ref, lng_ref, lnb_ref, yo_ref, *, N, cap):
    slot_id = lax.broadcasted_iota(I32, (cap, N), 0)
    onehot = jnp.concatenate(
        [jnp.where(slot_id == pos_ref[0, e:e + 1, :], 1.0, 0.0).astype(BF16) for e in range(N_EXPERTS)], axis=0)
    outs = out_ref[...].reshape(N_EXPERTS * cap, out_ref.shape[2])
    ff = lax.dot_general(onehot, outs, (((0,), (0,)), ((), ())), preferred_element_type=F32)
    yo_ref[0] = _layer_norm(ALPHA * y_ref[0] + mod_ref[0, 5:6, :] * ff, lng_ref[...], lnb_ref[...])


def _combine(pos, outs, y, mod, lng, lnb, per_batch):
    B, N, D = y.shape
    cap = (CAP_FACTOR * N) // N_EXPERTS
    return pl.pallas_call(
        functools.partial(_combine_kernel, N=N, cap=cap),
        grid=(B,),
        in_specs=[pl.BlockSpec((1, N_EXPERTS, N), lambda b: (b, 0, 0)),
                  pl.BlockSpec((N_EXPERTS, cap, D), lambda b: (0, b, 0)),
                  pl.BlockSpec((1, N, D), lambda b: (b, 0, 0)),
                  _mod_spec(per_batch, 1),
                  pl.BlockSpec((1, D), lambda b: (0, 0)),
                  pl.BlockSpec((1, D), lambda b: (0, 0))],
        out_specs=pl.BlockSpec((1, N, D), lambda b: (b, 0, 0)),
        out_shape=jax.ShapeDtypeStruct((B, N, D), F32),
        compiler_params=_cparams(1),
        name="combine",
    )(pos, outs, y, mod, lng.reshape(1, D), lnb.reshape(1, D))


def _to_heads(x, n_heads):
    B, L, _ = x.shape
    return x.reshape(B, L, n_heads, -1).transpose(0, 2, 1, 3)


def kernel(x_prompt, x_sample, state_hgrn, cache_diff_k, cache_diff_v, cache_win_k, cache_win_v, c, c_ctx, w_mod, b_mod, ln_g, ln_b, w_in_even, w_out_even, hgrn_lb_logits, hgrn_norm_g, diff_lambda, diff_norm_g, w_in_odd, w_out_odd, win_sink, w_router, w_exp_gate, w_exp_up, w_exp_down):
    yp, ys = x_prompt, x_sample
    nb_s = ys.shape[0]
    lb_all = jnp.cumsum(jax.nn.softmax(hgrn_lb_logits.astype(F32), axis=0), axis=0)
    cvec = jnp.zeros((16, D_MODEL), F32).at[:nb_s].set(c).at[nb_s].set(c_ctx)
    mods = _modulation(cvec, w_mod, b_mod).reshape(DEPTH, 16, 6, D_MODEL)
    mods = jnp.pad(mods, ((0, 0), (0, 0), (0, 2), (0, 0)))
    wr_f32 = jnp.pad(w_router, ((0, 0), (0, 0), (0, LANES - N_EXPERTS)))
    wr_hi = wr_f32.astype(BF16)
    wr_pad = jnp.concatenate([wr_hi, (wr_f32 - wr_hi.astype(F32)).astype(BF16)], axis=-1)
    new_state = new_dk = new_dv = new_wk = new_wv = None
    for l in range(DEPTH):
        mod_p = mods[l, nb_s:nb_s + 1]
        mod_s = mods[l, :nb_s]
        j = l // 2
        if l % 2 == 0:
            lam_init = 0.8 - 0.6 * math.exp(-0.3 * l)
            w_in = w_in_even[j].astype(BF16)
            w_out = w_out_even[j].astype(BF16)
            pp = _inproj(yp, mod_p, w_in, False)
            ps = _inproj(ys, mod_s, w_in, True)
            zero_state = jnp.zeros((yp.shape[0], 2, A_HEADS, A_DK, A_DV), F32)
            oa_p, st_p = _hgrn(pp, lb_all[l], hgrn_norm_g[j], zero_state)
            oa_s, _ = _hgrn(ps, lb_all[l], hgrn_norm_g[j], state_hgrn[:, j])
            ob_p = _diff_attn(pp, diff_lambda[j], diff_norm_g[j], lam_init)
            ob_s = _diff_attn(ps, diff_lambda[j], diff_norm_g[j], lam_init, ctx=(cache_diff_k, cache_diff_v, j))
            mix_p, mix_s = [oa_p, ob_p], [oa_s, ob_s]
            k0 = 3 * A_HEADS * A_DK + 2 * A_HEADS * A_DV + B_HEADS * 2 * B_DH
            new_state = st_p[:, None]
            new_dk = _to_heads(pp[..., k0:k0 + B_HEADS * 2 * B_DH], B_HEADS)[:, None]
            new_dv = _to_heads(pp[..., k0 + B_HEADS * 2 * B_DH:], B_HEADS)[:, None]
        else:
            w_in = w_in_odd[j].astype(BF16)
            w_out = w_out_odd[j].astype(BF16)
            pp = _inproj(yp, mod_p, w_in, False)
            ps = _inproj(ys, mod_s, w_in, True)
            mix_p = [_gqa(pp, win_sink[j])]
            mix_s = [_gqa(ps, win_sink[j], ctx=(cache_win_k, cache_win_v, j))]
            k0 = C_HEADS * C_DH
            new_wk = _to_heads(pp[..., k0:k0 + C_KV_HEADS * C_DH], C_KV_HEADS)[:, None]
            new_wv = _to_heads(pp[..., k0 + C_KV_HEADS * C_DH:], C_KV_HEADS)[:, None]
        yp, h2p, lgp = _outproj(yp, mix_p, mod_p, w_out, ln_g[l, 0], ln_b[l, 0], wr_pad[l], False)
        ys, h2s, lgs = _outproj(ys, mix_s, mod_s, w_out, ln_g[l, 0], ln_b[l, 0], wr_pad[l], True)
        pos_p, gs_p = _route(lgp)
        pos_s, gs_s = _route(lgs)
        xs_p = _gather(pos_p, h2p)
        xs_s = _gather(pos_s, h2s)
        out_p, out_s = _ffn(xs_p, xs_s, gs_p, gs_s, w_exp_gate, w_exp_up, w_exp_down, l)
        yp = _combine(pos_p, out_p, yp, mod_p, ln_g[l, 1], ln_b[l, 1], False)
        ys = _combine(pos_s, out_s, ys, mod_s, ln_g[l, 1], ln_b[l, 1], True)
    return (yp, ys, new_state, new_dk, new_dv, new_wk, new_wv)
```

```python
import functools
import math

import jax
import jax.numpy as jnp
from jax import lax
from jax.experimental import pallas as pl
from jax.experimental.pallas import tpu as pltpu

F32 = jnp.float32
BF16 = jnp.bfloat16
I32 = jnp.int32

D_MODEL = 1024
DEPTH = 2
GRID_W = 64
A_HEADS = 4
A_DK = 128
A_DV = 128
HGRN_CHUNK = 16
B_HEADS = 4
B_DH = 64
C_HEADS = 16
C_KV_HEADS = 4
C_GROUP = C_HEADS // C_KV_HEADS
C_DH = 64
WINDOW = 128
N_EXPERTS = 16
CAP_FACTOR = 2
ALPHA = (2 * DEPTH) ** 0.25
LN_EPS = 1e-5
ROPE_BASE = 10000.0
NEG_INF = -1e30
LOG2E = math.log2(math.e)
LANES = 128
VMEM_LIMIT = 56 * 1024 * 1024


def _cparams(n_axes):
    return pltpu.CompilerParams(dimension_semantics=("arbitrary",) * n_axes, vmem_limit_bytes=VMEM_LIMIT)


def _dot(a, b):
    return jnp.dot(a.astype(BF16), b.astype(BF16), preferred_element_type=F32)


def _dot_nt(a, b):
    return lax.dot_general(a.astype(BF16), b.astype(BF16), (((1,), (1,)), ((), ())), preferred_element_type=F32)


def _dot_tn(a, b):
    return lax.dot_general(a.astype(BF16), b.astype(BF16), (((0,), (0,)), ((), ())), preferred_element_type=F32)


def _split3(x):
    x1 = x.astype(BF16)
    r1 = x - x1.astype(F32)
    x2 = r1.astype(BF16)
    x3 = (r1 - x2.astype(F32)).astype(BF16)
    return x1, x2, x3


def _dot_sel(m, x):
    x1, x2, x3 = _split3(x)
    return (jnp.dot(m, x1, preferred_element_type=F32) + jnp.dot(m, x2, preferred_element_type=F32)
            + jnp.dot(m, x3, preferred_element_type=F32))


def _mod_kernel(c_ref, w_ref, b_ref, o_ref):
    a = jax.nn.silu(c_ref[...])
    w = w_ref[0]
    a_hi = a.astype(BF16)
    a_lo = (a - a_hi.astype(F32)).astype(BF16)
    w_hi = w.astype(BF16)
    w_lo = (w - w_hi.astype(F32)).astype(BF16)
    o_ref[0] = (jnp.dot(a_hi, w_hi, preferred_element_type=F32) + jnp.dot(a_hi, w_lo, preferred_element_type=F32)
                + jnp.dot(a_lo, w_hi, preferred_element_type=F32) + b_ref[0])


def _modulation(cvec, w_mod, b_mod):
    n, d = cvec.shape
    tn = 1024
    nt = w_mod.shape[2] // tn
    return pl.pallas_call(
        _mod_kernel,
        grid=(DEPTH, nt),
        in_specs=[pl.BlockSpec((n, d), lambda l, j: (0, 0)),
                  pl.BlockSpec((1, d, tn), lambda l, j: (l, 0, j)),
                  pl.BlockSpec((1, 1, tn), lambda l, j: (l, 0, j))],
        out_specs=pl.BlockSpec((1, n, tn), lambda l, j: (l, 0, j)),
        out_shape=jax.ShapeDtypeStruct((DEPTH, n, w_mod.shape[2]), F32),
        compiler_params=_cparams(2),
        name="modulation",
    )(cvec, w_mod, b_mod.reshape(DEPTH, 1, -1))


def _mod_spec(per_batch, n_axes):
    if n_axes == 1:
        return pl.BlockSpec((1, 8, D_MODEL), (lambda b: (b, 0, 0)) if per_batch else (lambda b: (0, 0, 0)))
    return pl.BlockSpec((1, 8, D_MODEL), (lambda b, i: (b, 0, 0)) if per_batch else (lambda b, i: (0, 0, 0)))


def _inproj_kernel(x_ref, mod_ref, w_ref, o_ref, *kv_refs, kv_col, n_heads, head_dim):
    h = x_ref[0] * (1.0 + mod_ref[0, 1:2, :]) + mod_ref[0, 0:1, :]
    p = jnp.dot(h.astype(BF16), w_ref[...], preferred_element_type=F32)
    o_ref[0] = p
    for t, kv_ref in enumerate(kv_refs):
        for hd in range(n_heads):
            c0 = kv_col + (t * n_heads + hd) * head_dim
            kv_ref[0, hd] = p[:, c0:c0 + head_dim]


def _inproj(x, mod, w_bf16, per_batch, kv=None):
    B, L, D = x.shape
    N = w_bf16.shape[1]
    tm = 256 if N > 2048 else min(L, 512)
    out_specs = [pl.BlockSpec((1, tm, N), lambda b, i: (b, i, 0))]
    out_shape = [jax.ShapeDtypeStruct((B, L, N), F32)]
    kv_col, n_heads, head_dim = kv if kv is not None else (0, 0, 0)
    if kv is not None:
        out_specs += [pl.BlockSpec((1, n_heads, tm, head_dim), lambda b, i: (b, 0, i, 0))] * 2
        out_shape += [jax.ShapeDtypeStruct((B, n_heads, L, head_dim), F32)] * 2
    return pl.pallas_call(
        functools.partial(_inproj_kernel, kv_col=kv_col, n_heads=n_heads, head_dim=head_dim),
        grid=(B, L // tm),
        in_specs=[pl.BlockSpec((1, tm, D), lambda b, i: (b, i, 0)),
                  _mod_spec(per_batch, 2),
                  pl.BlockSpec((D, N), lambda b, i: (0, 0))],
        out_specs=out_specs,
        out_shape=out_shape,
        compiler_params=_cparams(2),
        name="inproj",
    )(x, mod, w_bf16)


HGRN_HEADS_PER_STEP = 4
HGRN_PAIR = 2 * HGRN_CHUNK
HGRN_NPAIR = LANES // HGRN_PAIR


def _hgrn_kernel(q_ref, zf_ref, zb_ref, v_ref, g_ref, lb_ref, gn_ref, s0_ref, o_ref, sfin_ref,
                 oacc_scr, qx_scr, ut_scr, dec_scr, *, nblk):
    blk = LANES
    hps = HGRN_HEADS_PER_STEP
    npair = HGRN_NPAIR
    chains = [(hh, d) for hh in range(hps) for d in (0, 1)]
    gn = gn_ref[...]
    r = lax.broadcasted_iota(I32, (blk, blk), 0)
    c = lax.broadcasted_iota(I32, (blk, blk), 1)
    same = (r // HGRN_CHUNK) == (c // HGRN_CHUNK)
    rowpair = r // (2 * HGRN_CHUNK)
    row_even = (r // HGRN_CHUNK) % 2 == 0
    ones_f = jnp.where(same, 1.0, 0.0)
    tri = [jnp.logical_and(same, c <= r), jnp.logical_and(same, c >= r)]
    first = [row_even, jnp.logical_not(row_even)]
    other = jnp.logical_and(rowpair == c // (2 * HGRN_CHUNK), jnp.logical_not(same))
    adj = [jnp.logical_and(other, jnp.logical_not(f)) for f in first]
    sums_b = [jnp.concatenate([jnp.where(t, 1.0, 0.0), ones_f], axis=0).astype(BF16) for t in tri]
    z_refs = (zf_ref, zb_ref)

    def cols(hh):
        return slice(hh * LANES, (hh + 1) * LANES)

    def prepare(i, carry):
        rows = pl.ds(pl.multiple_of(i * blk, blk), blk)
        q = [q_ref[0, rows, cols(hh)] for hh in range(hps)]
        v = [v_ref[0, rows, cols(hh)] for hh in range(hps)]
        lb = [lb_ref[hh] for hh in range(hps)]
        z = {k: z_refs[k[1]][0, rows, cols(k[0])] for k in chains}
        logf = {k: jnp.log(lb[k[0]] + (1.0 - lb[k[0]]) * jax.nn.sigmoid(z[k])) for k in chains}
        key = {k: (1.0 - lb[k[0]]) * jax.nn.sigmoid(-z[k]) for k in chains}
        g_hi = {k: logf[k].astype(BF16) for k in chains}
        g_lo = {k: (logf[k] - g_hi[k].astype(F32)).astype(BF16) for k in chains}
        sums = {k: jnp.dot(sums_b[k[1]], jnp.concatenate([g_hi[k], g_lo[k]], axis=1), preferred_element_type=F32)
                for k in chains}
        sums = {k: sums[k][:, 0:A_DK] + sums[k][:, A_DK:2 * A_DK] for k in chains}
        b = {k: sums[k][0:blk] for k in chains}
        tot = {k: sums[k][blk:2 * blk] for k in chains}
        qd_f = {k: q[k[0]] * jnp.exp(b[k]) for k in chains}
        qd = {k: qd_f[k].astype(BF16) for k in chains}
        kd = {k: (key[k] * jnp.exp(-b[k])).astype(BF16) for k in chains}
        ke = {k: key[k] * jnp.exp(tot[k] - b[k]) for k in chains}
        dec = {k: jnp.exp(tot[k]) for k in chains}
        s_own = {k: _dot_nt(qd[k], kd[k]) for k in chains}
        s_adj = {k: _dot_nt(qd[k], ke[k]) for k in chains}
        partner = {k: jnp.where(row_even, jnp.concatenate([dec[k][HGRN_CHUNK:], dec[k][:HGRN_CHUNK]], axis=0),
                                jnp.concatenate([dec[k][-HGRN_CHUNK:], dec[k][:-HGRN_CHUNK]], axis=0)) for k in chains}
        ke2 = {k: jnp.where(first[k[1]], ke[k] * partner[k], ke[k]) for k in chains}
        ke_exp = {k: jnp.concatenate([jnp.where(rowpair == p, ke2[k], 0.0).astype(BF16) for p in range(npair)], axis=1)
                  for k in chains}
        vt = [v[hh].T.astype(BF16) for hh in range(hps)]
        for hh, d in chains:
            ut_scr[d, i, :, hh * npair * A_DK:(hh + 1) * npair * A_DK] = jnp.dot(vt[hh], ke_exp[(hh, d)],
                                                                                 preferred_element_type=F32)
        for hh, d in chains:
            k = (hh, d)
            qx_scr[d, rows, cols(hh)] = jnp.where(first[d], qd_f[k], qd_f[k] * partner[k]).astype(BF16)
            pair_dec = [dec[k][p * HGRN_PAIR:p * HGRN_PAIR + 1] * dec[k][p * HGRN_PAIR + HGRN_CHUNK:p * HGRN_PAIR + HGRN_CHUNK + 1]
                        for p in range(npair)]
            dec_scr[d, i, :, cols(hh)] = jnp.concatenate(pair_dec + [jnp.zeros((8 - npair, A_DK), F32)], axis=0)
        for hh in range(hps):
            both = None
            for d in (0, 1):
                s = jnp.where(tri[d], s_own[(hh, d)], jnp.where(adj[d], s_adj[(hh, d)], 0.0))
                both = s if both is None else both + s
            oacc_scr[rows, cols(hh)] = _dot(both, v[hh])
        return carry

    lax.fori_loop(0, nblk, prepare, 0)

    def sweep(i, sts):
        new = []
        for hh, d in chains:
            bi = i if d == 0 else nblk - 1 - i
            r0 = pl.multiple_of(bi * blk, blk)
            st = sts[hh * 2 + d]
            outs = [None] * npair
            for p in (range(npair) if d == 0 else reversed(range(npair))):
                outs[p] = _dot_nt(qx_scr[d, pl.ds(r0 + p * HGRN_PAIR, HGRN_PAIR), cols(hh)], st)
                st = (st * dec_scr[d, bi, p:p + 1, cols(hh)]
                      + ut_scr[d, bi, :, (hh * npair + p) * A_DK:(hh * npair + p + 1) * A_DK])
            rows = pl.ds(r0, blk)
            oacc_scr[rows, cols(hh)] = oacc_scr[rows, cols(hh)] + jnp.concatenate(outs, axis=0)
            new.append(st)
        return tuple(new)

    init = tuple(s0_ref[0, d, hh].T for hh, d in chains)
    sts = lax.fori_loop(0, nblk, sweep, init)
    for n, (hh, d) in enumerate(chains):
        sfin_ref[0, d, hh] = sts[n].T

    def finish(i, carry):
        rows = pl.ds(pl.multiple_of(i * blk, blk), blk)
        for hh in range(hps):
            t = oacc_scr[rows, cols(hh)]
            ms = jnp.mean(jnp.square(t), axis=-1, keepdims=True)
            o_ref[0, rows, cols(hh)] = (t * lax.rsqrt(ms + LN_EPS) * gn * jax.nn.silu(g_ref[0, rows, cols(hh)])).astype(BF16)
        return carry

    lax.fori_loop(0, nblk, finish, 0)


def _hgrn(p, lb, gn, s0):
    B, L, _ = p.shape
    H = A_HEADS
    hps = HGRN_HEADS_PER_STEP
    w = hps * LANES

    def col(off):
        return pl.BlockSpec((1, L, w), lambda b, h, off=off: (b, 0, off + h))

    st_spec = pl.BlockSpec((1, 2, hps, A_DK, A_DV), lambda b, h: (b, 0, h, 0, 0))
    ng = H // hps
    return pl.pallas_call(
        functools.partial(_hgrn_kernel, nblk=L // LANES),
        grid=(B, ng),
        in_specs=[col(0), col(ng), col(2 * ng), col(3 * ng), col(4 * ng),
                  pl.BlockSpec((hps, 1, A_DK), lambda b, h: (h, 0, 0)),
                  pl.BlockSpec((1, A_DV), lambda b, h: (0, 0)),
                  st_spec],
        out_specs=[pl.BlockSpec((1, L, w), lambda b, h: (b, 0, h)), st_spec],
        out_shape=[jax.ShapeDtypeStruct((B, L, H * A_DV), BF16), jax.ShapeDtypeStruct((B, 2, H, A_DK, A_DV), F32)],
        scratch_shapes=[pltpu.VMEM((L, w), F32), pltpu.VMEM((2, L, w), BF16),
                        pltpu.VMEM((2, L // LANES, A_DV, hps * HGRN_NPAIR * A_DK), F32),
                        pltpu.VMEM((2, L // LANES, 8, w), F32)],
        compiler_params=_cparams(2),
        name="hgrn",
    )(p, p, p, p, p, lb.reshape(H, 1, A_DK), gn.reshape(1, A_DV), s0)


def _rope_tables(L, width):
    quarter = C_DH // 4
    t = jnp.arange(L)
    rows = (t // GRID_W).astype(F32)
    cols = (t % GRID_W).astype(F32)
    inv = ROPE_BASE ** (-jnp.arange(quarter, dtype=F32) / quarter)
    ang_r = rows[:, None] * inv[None, :]
    ang_c = cols[:, None] * inv[None, :]
    cos = jnp.concatenate([jnp.cos(ang_r), jnp.cos(ang_r), jnp.cos(ang_c), jnp.cos(ang_c)], -1)
    sin = jnp.concatenate([-jnp.sin(ang_r), jnp.sin(ang_r), -jnp.sin(ang_c), jnp.sin(ang_c)], -1)
    reps = width // C_DH
    return jnp.tile(cos, (1, reps)), jnp.tile(sin, (1, reps))


def _rope(x, cos, sin):
    n = x.shape[-1]
    q = C_DH // 4
    lane = lax.broadcasted_iota(I32, x.shape, x.ndim - 1)
    partner = jnp.where((lane & q) == 0, pltpu.roll(x, n - q, axis=x.ndim - 1), pltpu.roll(x, q, axis=x.ndim - 1))
    return x * cos + partner * sin


def _diff_kernel(*refs, L, n_ctx, lam_init, tq):
    if n_ctx:
        q_ref, k_ref, v_ref, lam_ref, gn_ref, kc_ref, vc_ref, cos_ref, sin_ref, o_ref, kk_scr, vt_scr, s_scr = refs
    else:
        q_ref, k_ref, v_ref, lam_ref, gn_ref, o_ref, kk_scr, vt_scr, s_scr = refs
    lam = lam_ref[...]
    lam_full = (jnp.exp(jnp.sum(lam[0:1] * lam[1:2], keepdims=True)) - jnp.exp(jnp.sum(lam[2:3] * lam[3:4], keepdims=True))
                + lam_init)
    gn = gn_ref[...]
    lane_k = lax.broadcasted_iota(I32, (L, LANES), 1)
    k_lat = k_ref[0]
    if n_ctx:
        k_lat = _rope(k_lat, cos_ref[...], sin_ref[...])
        lane_c = lax.broadcasted_iota(I32, (n_ctx, LANES), 1)
        k_ctx = kc_ref[0, 0, 0]
        for i in (0, 1):
            kk_scr[i, 0:n_ctx, :] = jnp.where((lane_c >= B_DH) if i else (lane_c < B_DH), k_ctx, 0.0).astype(BF16)
        vt_scr[:, 0:n_ctx] = vc_ref[0, 0, 0].T.astype(BF16)
    for i in (0, 1):
        kk_scr[i, n_ctx:, :] = jnp.where((lane_k >= B_DH) if i else (lane_k < B_DH), k_lat, 0.0).astype(BF16)
    vt_scr[:, n_ctx:] = v_ref[0].T.astype(BF16)
    qscale = B_DH ** -0.5 * LOG2E

    def scores(j, slot):
        rows = slice(j * tq, (j + 1) * tq)
        q = q_ref[0, rows, :]
        if n_ctx:
            q = _rope(q, cos_ref[rows, :], sin_ref[rows, :])
        qb = (q * qscale).astype(BF16)
        for i in (0, 1):
            s_scr[slot, i] = lax.dot_general(kk_scr[i], qb, (((1,), (1,)), ((), ())), preferred_element_type=F32)

    def attend(j, slot):
        vt = vt_scr[...]
        outs = []
        for i in (0, 1):
            s = s_scr[slot, i]
            m = jnp.max(s, axis=0, keepdims=True)
            e = jnp.exp2(s - m)
            den = jnp.sum(e, axis=0, keepdims=True)
            outs.append(jnp.dot(vt, e.astype(BF16), preferred_element_type=F32) / den)
        o = outs[0] - lam_full * outs[1]
        ms = jnp.mean(jnp.square(o), axis=0, keepdims=True)
        o_ref[0, j * tq:(j + 1) * tq, :] = ((o * lax.rsqrt(ms + LN_EPS)).T * gn * (1.0 - lam_init)).astype(BF16)

    nq = L // tq
    scores(0, 0)
    for j in range(nq):
        if j + 1 < nq:
            scores(j + 1, (j + 1) % 2)
        attend(j, j % 2)


def _diff_attn(p, lam, gn, lam_init, ctx=None):
    B, L, _ = p.shape
    H = B_HEADS
    off = 5 * A_HEADS

    def col(o):
        return pl.BlockSpec((1, L, LANES), lambda b, h, o=o: (b, 0, o + h))

    in_specs = [col(off), col(off + H), col(off + 2 * H),
                pl.BlockSpec(lam.shape, lambda b, h: (0, 0)),
                pl.BlockSpec((1, LANES), lambda b, h: (0, 0))]
    args = [p, p, p, lam, gn.reshape(1, LANES)]
    n_ctx = 0
    if ctx is not None:
        kc, vc, j = ctx
        n_ctx = kc.shape[3]
        cos, sin = _rope_tables(L, LANES)
        in_specs += [pl.BlockSpec((1, 1, 1, n_ctx, LANES), lambda b, h, j=j: (b, j, h, 0, 0)),
                     pl.BlockSpec((1, 1, 1, n_ctx, LANES), lambda b, h, j=j: (b, j, h, 0, 0)),
                     pl.BlockSpec((L, LANES), lambda b, h: (0, 0)),
                     pl.BlockSpec((L, LANES), lambda b, h: (0, 0))]
        args += [kc, vc, cos, sin]
    tq = min(L, 256)
    return pl.pallas_call(
        functools.partial(_diff_kernel, L=L, n_ctx=n_ctx, lam_init=lam_init, tq=tq),
        grid=(B, H),
        in_specs=in_specs,
        out_specs=pl.BlockSpec((1, L, LANES), lambda b, h: (b, 0, h)),
        out_shape=jax.ShapeDtypeStruct((B, L, H * LANES), BF16),
        scratch_shapes=[pltpu.VMEM((2, n_ctx + L, LANES), BF16), pltpu.VMEM((LANES, n_ctx + L), BF16),
                        pltpu.VMEM((2, 2, n_ctx + L, tq), F32)],
        compiler_params=_cparams(2),
        name="diff_attn",
    )(*args)


def _gqa_place_keys(k2, kp_scr):
    lane = lax.broadcasted_iota(I32, k2.shape, 1)
    swapped = pltpu.roll(k2, C_DH, axis=1)
    for kh in (0, 1):
        for u in (0, 1):
            src = k2 if kh == u else swapped
            kp_scr[kh * 2 + u] = jnp.where((lane >= C_DH) if u else (lane < C_DH), src, 0.0).astype(BF16)


def _sink_row(sink_ref, h0, h1, n):
    lane = lax.broadcasted_iota(I32, (1, 2 * n), 1)
    return jnp.where(lane < n, sink_ref[h0], sink_ref[h1]) * LOG2E


def _q_pair_rows(qb, kh):
    c0 = kh * C_GROUP * C_DH
    return jnp.concatenate([qb[:, c0:c0 + LANES], qb[:, c0 + LANES:c0 + 2 * LANES]], axis=0)


def _gqa_full_kernel(sink_ref, q_ref, k_ref, v_ref, o_ref, kp_scr, *, L):
    hp = pl.program_id(1)
    _gqa_place_keys(k_ref[0], kp_scr)
    vt = v_ref[0].T.astype(BF16)
    qb = (q_ref[0] * (C_DH ** -0.5 * LOG2E)).astype(BF16)
    pieces = []
    for kh in (0, 1):
        h0 = (hp * 2 + kh) * C_GROUP
        rhs = _q_pair_rows(qb, kh)
        outs = [None] * C_GROUP
        for u in (0, 1):
            s = lax.dot_general(kp_scr[kh * 2 + u], rhs, (((1,), (1,)), ((), ())), preferred_element_type=F32)
            sink = _sink_row(sink_ref, h0 + u, h0 + 2 + u, L)
            m = jnp.maximum(jnp.max(s, axis=0, keepdims=True), sink)
            e = jnp.exp2(s - m)
            den = jnp.sum(e, axis=0, keepdims=True) + jnp.exp2(sink - m)
            ot = jnp.dot(vt[kh * C_DH:(kh + 1) * C_DH], e.astype(BF16), preferred_element_type=F32) / den
            outs[u], outs[2 + u] = ot[:, 0:L], ot[:, L:2 * L]
        pieces += outs
    o_ref[0] = jnp.concatenate(pieces, axis=0).T.astype(BF16)


def _gqa_win_kernel(sink_ref, q_ref, k_ref, v_ref, kc_ref, vc_ref, cosq_ref, sinq_ref, o_ref,
                    kp_scr, kcp_scr, vt_scr, vct_scr, sl_scr, sc_scr, *, L):
    hp = pl.program_id(1)
    tq = WINDOW
    nblk = L // tq
    qscale = C_DH ** -0.5 * LOG2E
    _gqa_place_keys(_rope(k_ref[0], cosq_ref[:, 0:LANES], sinq_ref[:, 0:LANES]), kp_scr)
    for kh in (0, 1):
        kc = kc_ref[0, 0, kh]
        zero = jnp.zeros_like(kc)
        kcp_scr[kh * 2] = jnp.concatenate([kc, zero], axis=1).astype(BF16)
        kcp_scr[kh * 2 + 1] = jnp.concatenate([zero, kc], axis=1).astype(BF16)
        vct_scr[kh] = vc_ref[0, 0, kh].T.astype(BF16)
    for j in range(nblk):
        vt_scr[j] = v_ref[0, j * tq:(j + 1) * tq, :].T.astype(BF16)

    k_in = lax.broadcasted_iota(I32, (tq, 2 * tq), 0)
    q_in = lax.broadcasted_iota(I32, (tq, 2 * tq), 1) % tq
    keep_prev = k_in >= q_in
    keep_next = k_in <= q_in

    def window(n):
        return [b for b in (n - 1, n, n + 1) if 0 <= b < nblk]

    def scores(n, slot):
        rows = slice(n * tq, (n + 1) * tq)
        qb = (_rope(q_ref[0, rows, :], cosq_ref[rows, :], sinq_ref[rows, :]) * qscale).astype(BF16)
        blocks = window(n)
        k0, nk = blocks[0] * tq, len(blocks) * tq
        for kh in (0, 1):
            rhs = _q_pair_rows(qb, kh)
            for u in (0, 1):
                sl_scr[slot, kh * 2 + u, 0:nk, :] = lax.dot_general(kp_scr[kh * 2 + u, k0:k0 + nk, :], rhs,
                                                                    (((1,), (1,)), ((), ())), preferred_element_type=F32)
                sc_scr[slot, kh * 2 + u] = lax.dot_general(kcp_scr[kh * 2 + u], rhs, (((1,), (1,)), ((), ())),
                                                           preferred_element_type=F32)

    def attend(n, slot):
        blocks = window(n)
        pieces = []
        for kh in (0, 1):
            h0 = (hp * 2 + kh) * C_GROUP
            vt_loc = jnp.concatenate([vt_scr[b, kh * C_DH:(kh + 1) * C_DH, :] for b in blocks], axis=1)
            outs = [None] * C_GROUP
            for u in (0, 1):
                parts = []
                for t, b in enumerate(blocks):
                    s = sl_scr[slot, kh * 2 + u, t * tq:(t + 1) * tq, :]
                    if b != n:
                        s = jnp.where(keep_prev if b < n else keep_next, s, NEG_INF)
                    parts.append(s)
                s_loc = jnp.concatenate(parts, axis=0)
                s_ctx = sc_scr[slot, kh * 2 + u]
                sink = _sink_row(sink_ref, h0 + u, h0 + 2 + u, tq)
                m = jnp.maximum(jnp.maximum(jnp.max(s_loc, axis=0, keepdims=True), jnp.max(s_ctx, axis=0, keepdims=True)), sink)
                e_loc = jnp.exp2(s_loc - m)
                e_ctx = jnp.exp2(s_ctx - m)
                den = jnp.sum(e_loc, axis=0, keepdims=True) + jnp.sum(e_ctx, axis=0, keepdims=True) + jnp.exp2(sink - m)
                ot = (jnp.dot(vt_loc, e_loc.astype(BF16), preferred_element_type=F32)
                      + jnp.dot(vct_scr[kh], e_ctx.astype(BF16), preferred_element_type=F32)) / den
                outs[u], outs[2 + u] = ot[:, 0:tq], ot[:, tq:2 * tq]
            pieces += outs
        o_ref[0, n * tq:(n + 1) * tq, :] = jnp.concatenate(pieces, axis=0).T.astype(BF16)

    scores(0, 0)
    for n in range(nblk):
        if n + 1 < nblk:
            scores(n + 1, (n + 1) % 2)
        attend(n, n % 2)


def _gqa(p, sink, ctx=None):
    B, L, _ = p.shape
    qw = 2 * C_GROUP * C_DH
    nq = C_HEADS * C_DH // LANES
    nk = C_KV_HEADS * C_DH // LANES
    in_specs = [pl.BlockSpec(memory_space=pltpu.SMEM),
                pl.BlockSpec((1, L, qw), lambda b, h: (b, 0, h)),
                pl.BlockSpec((1, L, LANES), lambda b, h: (b, 0, nq + h)),
                pl.BlockSpec((1, L, LANES), lambda b, h: (b, 0, nq + nk + h))]
    args = [sink, p, p, p]
    scratch = [pltpu.VMEM((4, L, LANES), BF16)]
    if ctx is None:
        body = functools.partial(_gqa_full_kernel, L=L)
    else:
        kc, vc, j = ctx
        n_ctx = kc.shape[3]
        cos, sin = _rope_tables(L, qw)
        in_specs += [pl.BlockSpec((1, 1, 2, n_ctx, C_DH), lambda b, h, j=j: (b, j, h, 0, 0)),
                     pl.BlockSpec((1, 1, 2, n_ctx, C_DH), lambda b, h, j=j: (b, j, h, 0, 0)),
                     pl.BlockSpec((L, qw), lambda b, h: (0, 0)),
                     pl.BlockSpec((L, qw), lambda b, h: (0, 0))]
        args += [kc, vc, cos, sin]
        scratch += [pltpu.VMEM((4, n_ctx, LANES), BF16), pltpu.VMEM((L // WINDOW, LANES, WINDOW), BF16),
                    pltpu.VMEM((2, C_DH, n_ctx), BF16),
                    pltpu.VMEM((2, 4, 3 * WINDOW, 2 * WINDOW), F32), pltpu.VMEM((2, 4, n_ctx, 2 * WINDOW), F32)]
        body = functools.partial(_gqa_win_kernel, L=L)
    return pl.pallas_call(
        body,
        grid=(B, C_KV_HEADS // 2),
        in_specs=in_specs,
        out_specs=pl.BlockSpec((1, L, qw), lambda b, h: (b, 0, h)),
        out_shape=jax.ShapeDtypeStruct((B, L, C_HEADS * C_DH), BF16),
        scratch_shapes=scratch,
        compiler_params=_cparams(2),
        name="gqa",
    )(*args)


def _layer_norm(t, g, b):
    mu = jnp.mean(t, axis=-1, keepdims=True)
    tc = t - mu
    var = jnp.mean(jnp.square(tc), axis=-1, keepdims=True)
    return tc * lax.rsqrt(var + LN_EPS) * g + b


def _outproj_kernel(*refs, n_in):
    y_ref = refs[0]
    a_refs = refs[1:1 + n_in]
    mod_ref, w_ref, lng_ref, lnb_ref, wr_ref, yo_ref, h2_ref, lg_ref = refs[1 + n_in:]
    mix = None
    k0 = 0
    for a_ref in a_refs:
        kw = a_ref.shape[2]
        part = jnp.dot(a_ref[0].astype(BF16), w_ref[k0:k0 + kw, :], preferred_element_type=F32)
        mix = part if mix is None else mix + part
        k0 += kw
    yn = _layer_norm(ALPHA * y_ref[0] + mod_ref[0, 2:3, :] * mix, lng_ref[...], lnb_ref[...])
    yo_ref[0] = yn
    h2 = yn * (1.0 + mod_ref[0, 4:5, :]) + mod_ref[0, 3:4, :]
    h_hi = h2.astype(BF16)
    h2_ref[0] = h_hi
    h_lo = (h2 - h_hi.astype(F32)).astype(BF16)
    both = jnp.dot(h_hi, wr_ref[...], preferred_element_type=F32)
    lg_ref[0] = (both[:, 0:LANES] + both[:, LANES:2 * LANES]
                 + jnp.dot(h_lo, wr_ref[:, 0:LANES], preferred_element_type=F32))


def _outproj(y, mixes, mod, w_bf16, lng, lnb, wr_pad, per_batch):
    B, L, D = y.shape
    tm = min(L, 512)
    n_in = len(mixes)
    in_specs = [pl.BlockSpec((1, tm, D), lambda b, i: (b, i, 0))]
    in_specs += [pl.BlockSpec((1, tm, m.shape[2]), lambda b, i: (b, i, 0)) for m in mixes]
    in_specs += [_mod_spec(per_batch, 2),
                 pl.BlockSpec(w_bf16.shape, lambda b, i: (0, 0)),
                 pl.BlockSpec((1, D), lambda b, i: (0, 0)),
                 pl.BlockSpec((1, D), lambda b, i: (0, 0)),
                 pl.BlockSpec((D, 2 * LANES), lambda b, i: (0, 0))]
    return pl.pallas_call(
        functools.partial(_outproj_kernel, n_in=n_in),
        grid=(B, L // tm),
        in_specs=in_specs,
        out_specs=[pl.BlockSpec((1, tm, D), lambda b, i: (b, i, 0)),
                   pl.BlockSpec((1, tm, D), lambda b, i: (b, i, 0)),
                   pl.BlockSpec((1, tm, LANES), lambda b, i: (b, i, 0))],
        out_shape=[jax.ShapeDtypeStruct((B, L, D), F32), jax.ShapeDtypeStruct((B, L, D), BF16),
                   jax.ShapeDtypeStruct((B, L, LANES), F32)],
        compiler_params=_cparams(2),
        name="outproj",
    )(y, *mixes, mod, w_bf16, lng.reshape(1, D), lnb.reshape(1, D), wr_pad)


def _route_kernel(lg_ref, upper_ref, pos_ref, gs_ref, aff_scr, thr_scr, *, B, N, cap):
    E = N_EXPERTS

    def affinities(b, carry):
        lg = lg_ref[b]
        lane = lax.broadcasted_iota(I32, lg.shape, 1)
        lgm = jnp.where(lane < E, lg, -jnp.inf)
        m = jnp.max(lgm, axis=-1, keepdims=True)
        ex = jnp.exp(lgm - m)
        aff = ex / jnp.sum(ex, axis=-1, keepdims=True)
        aff_scr[pl.ds(pl.multiple_of(b * E, E), E), :] = aff.T[0:E, :]
        return carry

    lax.fori_loop(0, B, affinities, 0)
    aff_all = aff_scr[...]

    def bisect(_, carry):
        lo, hi = carry
        mid = lo + jnp.right_shift(hi - lo + 1, 1)
        cnt = jnp.sum((aff_all >= pltpu.bitcast(mid, F32)).astype(I32), axis=-1, keepdims=True)
        ok = cnt >= cap
        return jnp.where(ok, mid, lo), jnp.where(ok, hi, mid - 1)

    lo0 = jnp.zeros((B * E, 1), I32)
    hi0 = jnp.full((B * E, 1), 0x7F800000, I32)
    thr, _ = lax.fori_loop(0, 31, bisect, (lo0, hi0))
    thr_scr[...] = jnp.broadcast_to(thr, (B * E, LANES))

    def select(b, carry):
        rows = pl.ds(pl.multiple_of(b * E, E), E)
        aff_t = aff_scr[rows, :]
        thr_b = pltpu.bitcast(thr_scr[rows, 0:1], F32)
        gt = aff_t > thr_b
        eq = aff_t == thr_b
        need = cap - jnp.sum(gt.astype(I32), axis=-1, keepdims=True)
        upper = upper_ref[...]
        tie_rank = jnp.dot(jnp.where(eq, 1.0, 0.0).astype(BF16), upper, preferred_element_type=F32)
        sel = jnp.logical_or(gt, jnp.logical_and(eq, tie_rank < need.astype(F32)))
        slot = jnp.dot(jnp.where(sel, 1.0, 0.0).astype(BF16), upper, preferred_element_type=F32)
        pos = jnp.where(sel, slot.astype(I32), -1)
        pos_ref[b] = pos
        slot_id = lax.broadcasted_iota(I32, (cap, N), 0)
        for e in range(E):
            hit = slot_id == pos[e:e + 1, :]
            gcol = jnp.sum(jnp.where(hit, aff_t[e:e + 1, :], 0.0), axis=-1, keepdims=True)
            gs_ref[b, e] = jnp.broadcast_to(gcol, (cap, LANES))
        return carry

    lax.fori_loop(0, B, select, 0)


def _route(logits):
    B, N, _ = logits.shape
    E = N_EXPERTS
    cap = (CAP_FACTOR * N) // E
    upper = (jnp.arange(N)[:, None] < jnp.arange(N)[None, :]).astype(BF16)
    return pl.pallas_call(
        functools.partial(_route_kernel, B=B, N=N, cap=cap),
        grid=(1,),
        in_specs=[pl.BlockSpec((B, N, LANES), lambda i: (0, 0, 0)),
                  pl.BlockSpec((N, N), lambda i: (0, 0))],
        out_specs=[pl.BlockSpec((B, E, N), lambda i: (0, 0, 0)),
                   pl.BlockSpec((B, E, cap, LANES), lambda i: (0, 0, 0, 0))],
        out_shape=[jax.ShapeDtypeStruct((B, E, N), I32),
                   jax.ShapeDtypeStruct((B, E, cap, LANES), F32)],
        scratch_shapes=[pltpu.VMEM((B * E, N), F32), pltpu.VMEM((B * E, LANES), I32)],
        compiler_params=_cparams(1),
        name="route",
    )(logits, upper)


def _gather_kernel(pos_ref, h_ref, xs_ref, *, N, cap):
    slot_id = lax.broadcasted_iota(I32, (cap, N), 0)
    onehot = jnp.concatenate(
        [jnp.where(slot_id == pos_ref[0, e:e + 1, :], 1.0, 0.0).astype(BF16) for e in range(N_EXPERTS)], axis=0)
    xs = jnp.dot(onehot, h_ref[0], preferred_element_type=F32).astype(BF16)
    xs_ref[...] = xs.reshape(N_EXPERTS, cap, xs.shape[1])


def _gather(pos, h2):
    B, N, D = h2.shape
    cap = (CAP_FACTOR * N) // N_EXPERTS
    return pl.pallas_call(
        functools.partial(_gather_kernel, N=N, cap=cap),
        grid=(B,),
        in_specs=[pl.BlockSpec((1, N_EXPERTS, N), lambda b: (b, 0, 0)),
                  pl.BlockSpec((1, N, D), lambda b: (b, 0, 0))],
        out_specs=pl.BlockSpec((N_EXPERTS, cap, D), lambda b: (0, b, 0)),
        out_shape=jax.ShapeDtypeStruct((N_EXPERTS, B * cap, D), BF16),
        compiler_params=_cparams(1),
        name="gather",
    )(pos, h2)


def _ffn_kernel(xp_ref, xs_ref, gp_ref, gs_ref, wg_ref, wu_ref, wd_ref, op_ref, os_ref, *, tr):
    wg = wg_ref[0, 0].astype(BF16)
    wu = wu_ref[0, 0].astype(BF16)
    wd = wd_ref[0, 0].astype(BF16)
    for x_ref, g_ref, o_ref in ((xp_ref, gp_ref, op_ref), (xs_ref, gs_ref, os_ref)):
        rows = x_ref.shape[1]
        cap = g_ref.shape[2]
        for r0 in range(0, rows, tr):
            x = x_ref[0, r0:r0 + tr, :]
            hid = jax.nn.silu(jnp.dot(x, wg, preferred_element_type=F32)) * jnp.dot(x, wu, preferred_element_type=F32)
            out = jnp.dot(hid.astype(BF16), wd, preferred_element_type=F32)
            gate = g_ref[r0 // cap:(r0 + tr) // cap, 0].reshape(tr, LANES)[:, 0:1]
            o_ref[0, r0:r0 + tr, :] = (out * gate).astype(BF16)


def _ffn(xs_p, xs_s, gs_p, gs_s, wg, wu, wd, l):
    E, rp, D = xs_p.shape
    rs = xs_s.shape[1]
    FF = wg.shape[3]
    tr = 512

    def gspec(g):
        return pl.BlockSpec((g.shape[0], 1, g.shape[2], LANES), lambda e: (0, e, 0, 0))

    return pl.pallas_call(
        functools.partial(_ffn_kernel, tr=tr),
        grid=(E,),
        in_specs=[pl.BlockSpec((1, rp, D), lambda e: (e, 0, 0)),
                  pl.BlockSpec((1, rs, D), lambda e: (e, 0, 0)),
                  gspec(gs_p), gspec(gs_s),
                  pl.BlockSpec((1, 1, D, FF), lambda e, l=l: (l, e, 0, 0)),
                  pl.BlockSpec((1, 1, D, FF), lambda e, l=l: (l, e, 0, 0)),
                  pl.BlockSpec((1, 1, FF, D), lambda e, l=l: (l, e, 0, 0))],
        out_specs=[pl.BlockSpec((1, rp, D), lambda e: (e, 0, 0)),
                   pl.BlockSpec((1, rs, D), lambda e: (e, 0, 0))],
        out_shape=[jax.ShapeDtypeStruct((E, rp, D), BF16), jax.ShapeDtypeStruct((E, rs, D), BF16)],
        compiler_params=_cparams(1),
        name="ffn",
    )(xs_p, xs_s, gs_p, gs_s, wg, wu, wd)


def _combine_kernel(pos_ref, out_ref, y_ref, mod_ref, lng_ref, lnb_ref, yo_ref, *, N, cap):
    slot_id = lax.broadcasted_iota(I32, (cap, N), 0)
    onehot = jnp.concatenate(
        [jnp.where(slot_id == pos_ref[0, e:e + 1, :], 1.0, 0.0).astype(BF16) for e in range(N_EXPERTS)], axis=0)
    outs = out_ref[...].reshape(N_EXPERTS * cap, out_ref.shape[2])
    ff = lax.dot_general(onehot, outs, (((0,), (0,)), ((), ())), preferred_element_type=F32)
    yo_ref[0] = _layer_norm(ALPHA * y_ref[0] + mod_ref[0, 5:6, :] * ff, lng_ref[...], lnb_ref[...])


def _combine(pos, outs, y, mod, lng, lnb, per_batch):
    B, N, D = y.shape
    cap = (CAP_FACTOR * N) // N_EXPERTS
    return pl.pallas_call(
        functools.partial(_combine_kernel, N=N, cap=cap),
        grid=(B,),
        in_specs=[pl.BlockSpec((1, N_EXPERTS, N), lambda b: (b, 0, 0)),
                  pl.BlockSpec((N_EXPERTS, cap, D), lambda b: (0, b, 0)),
                  pl.BlockSpec((1, N, D), lambda b: (b, 0, 0)),
                  _mod_spec(per_batch, 1),
                  pl.BlockSpec((1, D), lambda b: (0, 0)),
                  pl.BlockSpec((1, D), lambda b: (0, 0))],
        out_specs=pl.BlockSpec((1, N, D), lambda b: (b, 0, 0)),
        out_shape=jax.ShapeDtypeStruct((B, N, D), F32),
        compiler_params=_cparams(1),
        name="combine",
    )(pos, outs, y, mod, lng.reshape(1, D), lnb.reshape(1, D))


def kernel(x_prompt, x_sample, state_hgrn, cache_diff_k, cache_diff_v, cache_win_k, cache_win_v, c, c_ctx, w_mod, b_mod, ln_g, ln_b, w_in_even, w_out_even, hgrn_lb_logits, hgrn_norm_g, diff_lambda, diff_norm_g, w_in_odd, w_out_odd, win_sink, w_router, w_exp_gate, w_exp_up, w_exp_down):
    yp, ys = x_prompt, x_sample
    nb_s = ys.shape[0]
    lb_all = jnp.cumsum(jax.nn.softmax(hgrn_lb_logits.astype(F32), axis=0), axis=0)
    cvec = jnp.zeros((16, D_MODEL), F32).at[:nb_s].set(c).at[nb_s].set(c_ctx)
    mods = _modulation(cvec, w_mod, b_mod).reshape(DEPTH, 16, 6, D_MODEL)
    mods = jnp.pad(mods, ((0, 0), (0, 0), (0, 2), (0, 0)))
    wr_f32 = jnp.pad(w_router, ((0, 0), (0, 0), (0, LANES - N_EXPERTS)))
    wr_hi = wr_f32.astype(BF16)
    wr_pad = jnp.concatenate([wr_hi, (wr_f32 - wr_hi.astype(F32)).astype(BF16)], axis=-1)
    new_state = new_dk = new_dv = new_wk = new_wv = None
    for l in range(DEPTH):
        mod_p = mods[l, nb_s:nb_s + 1]
        mod_s = mods[l, :nb_s]
        j = l // 2
        if l % 2 == 0:
            lam_init = 0.8 - 0.6 * math.exp(-0.3 * l)
            w_in = w_in_even[j].astype(BF16)
            w_out = w_out_even[j].astype(BF16)
            k0 = 3 * A_HEADS * A_DK + 2 * A_HEADS * A_DV + B_HEADS * 2 * B_DH
            pp, k_new, v_new = _inproj(yp, mod_p, w_in, False, kv=(k0, B_HEADS, 2 * B_DH))
            ps, = _inproj(ys, mod_s, w_in, True)
            new_dk, new_dv = k_new[:, None], v_new[:, None]
            zero_state = jnp.zeros((yp.shape[0], 2, A_HEADS, A_DK, A_DV), F32)
            oa_p, st_p = _hgrn(pp, lb_all[l], hgrn_norm_g[j], zero_state)
            oa_s, _ = _hgrn(ps, lb_all[l], hgrn_norm_g[j], state_hgrn[:, j])
            ob_p = _diff_attn(pp, diff_lambda[j], diff_norm_g[j], lam_init)
            ob_s = _diff_attn(ps, diff_lambda[j], diff_norm_g[j], lam_init, ctx=(cache_diff_k, cache_diff_v, j))
            mix_p, mix_s = [oa_p, ob_p], [oa_s, ob_s]
            new_state = st_p[:, None]
        else:
            w_in = w_in_odd[j].astype(BF16)
            w_out = w_out_odd[j].astype(BF16)
            pp, k_new, v_new = _inproj(yp, mod_p, w_in, False, kv=(C_HEADS * C_DH, C_KV_HEADS, C_DH))
            ps, = _inproj(ys, mod_s, w_in, True)
            new_wk, new_wv = k_new[:, None], v_new[:, None]
            mix_p = [_gqa(pp, win_sink[j])]
            mix_s = [_gqa(ps, win_sink[j], ctx=(cache_win_k, cache_win_v, j))]
        yp, h2p, lgp = _outproj(yp, mix_p, mod_p, w_out, ln_g[l, 0], ln_b[l, 0], wr_pad[l], False)
        ys, h2s, lgs = _outproj(ys, mix_s, mod_s, w_out, ln_g[l, 0], ln_b[l, 0], wr_pad[l], True)
        pos_p, gs_p = _route(lgp)
        pos_s, gs_s = _route(lgs)
        xs_p = _gather(pos_p, h2p)
        xs_s = _gather(pos_s, h2s)
        out_p, out_s = _ffn(xs_p, xs_s, gs_p, gs_s, w_exp_gate, w_exp_up, w_exp_down, l)
        yp = _combine(pos_p, out_p, yp, mod_p, ln_g[l, 1], ln_b[l, 1], False)
        ys = _combine(pos_s, out_s, ys, mod_s, ln_g[l, 1], ln_b[l, 1], True)
    return (yp, ys, new_state, new_dk, new_dv, new_wk, new_wv)
```

```python
import functools
import math

import jax
import jax.numpy as jnp
from jax import lax
from jax.experimental import pallas as pl
from jax.experimental.pallas import tpu as pltpu

F32 = jnp.float32
BF16 = jnp.bfloat16
I32 = jnp.int32

D_MODEL = 1024
DEPTH = 2
GRID_W = 64
A_HEADS = 4
A_DK = 128
A_DV = 128
HGRN_CHUNK = 16
B_HEADS = 4
B_DH = 64
C_HEADS = 16
C_KV_HEADS = 4
C_GROUP = C_HEADS // C_KV_HEADS
C_DH = 64
WINDOW = 128
N_EXPERTS = 16
CAP_FACTOR = 2
ALPHA = (2 * DEPTH) ** 0.25
LN_EPS = 1e-5
ROPE_BASE = 10000.0
NEG_INF = -1e30
LOG2E = math.log2(math.e)
LANES = 128
VMEM_LIMIT = 56 * 1024 * 1024
MOD_COLS = 1024
PROJ_ROWS = 512
PROJ_ROWS_WIDE = 256
PROJ_WIDE = 2048
ATTN_QUERIES = 256
FFN_ROWS = 512


def _cparams(n_axes):
    return pltpu.CompilerParams(dimension_semantics=("arbitrary",) * n_axes, vmem_limit_bytes=VMEM_LIMIT)


def _dot(a, b):
    return jnp.dot(a.astype(BF16), b.astype(BF16), preferred_element_type=F32)


def _dot_nt(a, b):
    return lax.dot_general(a.astype(BF16), b.astype(BF16), (((1,), (1,)), ((), ())), preferred_element_type=F32)


def _dot_tn(a, b):
    return lax.dot_general(a.astype(BF16), b.astype(BF16), (((0,), (0,)), ((), ())), preferred_element_type=F32)


def _split3(x):
    x1 = x.astype(BF16)
    r1 = x - x1.astype(F32)
    x2 = r1.astype(BF16)
    x3 = (r1 - x2.astype(F32)).astype(BF16)
    return x1, x2, x3


def _dot_sel(m, x):
    x1, x2, x3 = _split3(x)
    return (jnp.dot(m, x1, preferred_element_type=F32) + jnp.dot(m, x2, preferred_element_type=F32)
            + jnp.dot(m, x3, preferred_element_type=F32))


def _mod_kernel(c_ref, w_ref, b_ref, o_ref):
    a = jax.nn.silu(c_ref[...])
    w = w_ref[0]
    a_hi = a.astype(BF16)
    a_lo = (a - a_hi.astype(F32)).astype(BF16)
    w_hi = w.astype(BF16)
    w_lo = (w - w_hi.astype(F32)).astype(BF16)
    n = a.shape[0]
    both = jnp.dot(jnp.concatenate([a_hi, a_lo], axis=0), w_hi, preferred_element_type=F32)
    o_ref[0] = both[0:n] + both[n:2 * n] + jnp.dot(a_hi, w_lo, preferred_element_type=F32) + b_ref[0]


def _modulation(cvec, w_mod, b_mod):
    n, d = cvec.shape
    tn = MOD_COLS
    nt = w_mod.shape[2] // tn
    return pl.pallas_call(
        _mod_kernel,
        grid=(DEPTH, nt),
        in_specs=[pl.BlockSpec((n, d), lambda l, j: (0, 0)),
                  pl.BlockSpec((1, d, tn), lambda l, j: (l, 0, j)),
                  pl.BlockSpec((1, 1, tn), lambda l, j: (l, 0, j))],
        out_specs=pl.BlockSpec((1, n, tn), lambda l, j: (l, 0, j)),
        out_shape=jax.ShapeDtypeStruct((DEPTH, n, w_mod.shape[2]), F32),
        compiler_params=_cparams(2),
        name="modulation",
    )(cvec, w_mod, b_mod.reshape(DEPTH, 1, -1))


def _mod_spec(per_batch, n_axes):
    if n_axes == 1:
        return pl.BlockSpec((1, 8, D_MODEL), (lambda b: (b, 0, 0)) if per_batch else (lambda b: (0, 0, 0)))
    return pl.BlockSpec((1, 8, D_MODEL), (lambda b, i: (b, 0, 0)) if per_batch else (lambda b, i: (0, 0, 0)))


def _inproj_kernel(x_ref, mod_ref, w_ref, o_ref, *kv_refs, kv_col, n_heads, head_dim):
    h = x_ref[0] * (1.0 + mod_ref[0, 1:2, :]) + mod_ref[0, 0:1, :]
    p = jnp.dot(h.astype(BF16), w_ref[...], preferred_element_type=F32)
    o_ref[0] = p
    for t, kv_ref in enumerate(kv_refs):
        for hd in range(n_heads):
            c0 = kv_col + (t * n_heads + hd) * head_dim
            kv_ref[0, hd] = p[:, c0:c0 + head_dim]


def _inproj(x, mod, w_bf16, per_batch, kv=None):
    B, L, D = x.shape
    N = w_bf16.shape[1]
    tm = PROJ_ROWS_WIDE if N > PROJ_WIDE else min(L, PROJ_ROWS)
    out_specs = [pl.BlockSpec((1, tm, N), lambda b, i: (b, i, 0))]
    out_shape = [jax.ShapeDtypeStruct((B, L, N), F32)]
    kv_col, n_heads, head_dim = kv if kv is not None else (0, 0, 0)
    if kv is not None:
        out_specs += [pl.BlockSpec((1, n_heads, tm, head_dim), lambda b, i: (b, 0, i, 0))] * 2
        out_shape += [jax.ShapeDtypeStruct((B, n_heads, L, head_dim), F32)] * 2
    return pl.pallas_call(
        functools.partial(_inproj_kernel, kv_col=kv_col, n_heads=n_heads, head_dim=head_dim),
        grid=(B, L // tm),
        in_specs=[pl.BlockSpec((1, tm, D), lambda b, i: (b, i, 0)),
                  _mod_spec(per_batch, 2),
                  pl.BlockSpec((D, N), lambda b, i: (0, 0))],
        out_specs=out_specs,
        out_shape=out_shape,
        compiler_params=_cparams(2),
        name="inproj",
    )(x, mod, w_bf16)


HGRN_HEADS_PER_STEP = 4
HGRN_PAIR = 2 * HGRN_CHUNK
HGRN_NPAIR = LANES // HGRN_PAIR


def _hgrn_kernel(q_ref, zf_ref, zb_ref, v_ref, g_ref, lb_ref, gn_ref, s0_ref, o_ref, sfin_ref,
                 oacc_scr, qx_scr, ut_scr, dec_scr, *, nblk):
    blk = LANES
    hps = HGRN_HEADS_PER_STEP
    npair = HGRN_NPAIR
    chains = [(hh, d) for hh in range(hps) for d in (0, 1)]
    gn = gn_ref[...]
    r = lax.broadcasted_iota(I32, (blk, blk), 0)
    c = lax.broadcasted_iota(I32, (blk, blk), 1)
    same = (r // HGRN_CHUNK) == (c // HGRN_CHUNK)
    rowpair = r // (2 * HGRN_CHUNK)
    row_even = (r // HGRN_CHUNK) % 2 == 0
    ones_f = jnp.where(same, 1.0, 0.0)
    tri = [jnp.logical_and(same, c <= r), jnp.logical_and(same, c >= r)]
    first = [row_even, jnp.logical_not(row_even)]
    other = jnp.logical_and(rowpair == c // (2 * HGRN_CHUNK), jnp.logical_not(same))
    adj = [jnp.logical_and(other, jnp.logical_not(f)) for f in first]
    sums_b = [jnp.concatenate([jnp.where(t, 1.0, 0.0), ones_f], axis=0).astype(BF16) for t in tri]
    z_refs = (zf_ref, zb_ref)

    def cols(hh):
        return slice(hh * LANES, (hh + 1) * LANES)

    def prepare(i, carry):
        rows = pl.ds(pl.multiple_of(i * blk, blk), blk)
        q = [q_ref[0, rows, cols(hh)] for hh in range(hps)]
        v = [v_ref[0, rows, cols(hh)] for hh in range(hps)]
        lb = [lb_ref[hh] for hh in range(hps)]
        z = {k: z_refs[k[1]][0, rows, cols(k[0])] for k in chains}
        logf = {k: jnp.log(lb[k[0]] + (1.0 - lb[k[0]]) * jax.nn.sigmoid(z[k])) for k in chains}
        key = {k: (1.0 - lb[k[0]]) * jax.nn.sigmoid(-z[k]) for k in chains}
        g_hi = {k: logf[k].astype(BF16) for k in chains}
        g_lo = {k: (logf[k] - g_hi[k].astype(F32)).astype(BF16) for k in chains}
        sums = {k: jnp.dot(sums_b[k[1]], jnp.concatenate([g_hi[k], g_lo[k]], axis=1), preferred_element_type=F32)
                for k in chains}
        sums = {k: sums[k][:, 0:A_DK] + sums[k][:, A_DK:2 * A_DK] for k in chains}
        b = {k: sums[k][0:blk] for k in chains}
        tot = {k: sums[k][blk:2 * blk] for k in chains}
        qd_f = {k: q[k[0]] * jnp.exp(b[k]) for k in chains}
        qd = {k: qd_f[k].astype(BF16) for k in chains}
        kd = {k: (key[k] * jnp.exp(-b[k])).astype(BF16) for k in chains}
        ke = {k: key[k] * jnp.exp(tot[k] - b[k]) for k in chains}
        dec = {k: jnp.exp(tot[k]) for k in chains}
        s_two = {k: _dot_nt(qd[k], jnp.concatenate([kd[k], ke[k].astype(BF16)], axis=0)) for k in chains}
        s_own = {k: s_two[k][:, 0:blk] for k in chains}
        s_adj = {k: s_two[k][:, blk:2 * blk] for k in chains}
        partner = {k: jnp.where(row_even, jnp.concatenate([dec[k][HGRN_CHUNK:], dec[k][:HGRN_CHUNK]], axis=0),
                                jnp.concatenate([dec[k][-HGRN_CHUNK:], dec[k][:-HGRN_CHUNK]], axis=0)) for k in chains}
        ke2 = {k: jnp.where(first[k[1]], ke[k] * partner[k], ke[k]) for k in chains}
        ke_exp = {k: jnp.concatenate([jnp.where(rowpair == p, ke2[k], 0.0).astype(BF16) for p in range(npair)], axis=1)
                  for k in chains}
        vt = [v[hh].T.astype(BF16) for hh in range(hps)]
        for hh, d in chains:
            ut_scr[d, i, :, hh * npair * A_DK:(hh + 1) * npair * A_DK] = jnp.dot(vt[hh], ke_exp[(hh, d)],
                                                                                 preferred_element_type=F32)
        for hh, d in chains:
            k = (hh, d)
            qx_scr[d, rows, cols(hh)] = jnp.where(first[d], qd_f[k], qd_f[k] * partner[k]).astype(BF16)
            pair_dec = [dec[k][p * HGRN_PAIR:p * HGRN_PAIR + 1] * dec[k][p * HGRN_PAIR + HGRN_CHUNK:p * HGRN_PAIR + HGRN_CHUNK + 1]
                        for p in range(npair)]
            dec_scr[d, i, :, cols(hh)] = jnp.concatenate(pair_dec + [jnp.zeros((8 - npair, A_DK), F32)], axis=0)
        for hh in range(hps):
            both = None
            for d in (0, 1):
                s = jnp.where(tri[d], s_own[(hh, d)], jnp.where(adj[d], s_adj[(hh, d)], 0.0))
                both = s if both is None else both + s
            oacc_scr[rows, cols(hh)] = _dot(both, v[hh])
        return carry

    lax.fori_loop(0, nblk, prepare, 0)

    def sweep(i, sts):
        new = []
        for hh, d in chains:
            bi = i if d == 0 else nblk - 1 - i
            r0 = pl.multiple_of(bi * blk, blk)
            st = sts[hh * 2 + d]
            outs = [None] * npair
            for p in (range(npair) if d == 0 else reversed(range(npair))):
                outs[p] = _dot_nt(qx_scr[d, pl.ds(r0 + p * HGRN_PAIR, HGRN_PAIR), cols(hh)], st)
                st = (st * dec_scr[d, bi, p:p + 1, cols(hh)]
                      + ut_scr[d, bi, :, (hh * npair + p) * A_DK:(hh * npair + p + 1) * A_DK])
            rows = pl.ds(r0, blk)
            oacc_scr[rows, cols(hh)] = oacc_scr[rows, cols(hh)] + jnp.concatenate(outs, axis=0)
            new.append(st)
        return tuple(new)

    init = tuple(s0_ref[0, d, hh].T for hh, d in chains)
    sts = lax.fori_loop(0, nblk, sweep, init)
    for n, (hh, d) in enumerate(chains):
        sfin_ref[0, d, hh] = sts[n].T

    def finish(i, carry):
        rows = pl.ds(pl.multiple_of(i * blk, blk), blk)
        for hh in range(hps):
            t = oacc_scr[rows, cols(hh)]
            ms = jnp.mean(jnp.square(t), axis=-1, keepdims=True)
            o_ref[0, rows, cols(hh)] = (t * lax.rsqrt(ms + LN_EPS) * gn * jax.nn.silu(g_ref[0, rows, cols(hh)])).astype(BF16)
        return carry

    lax.fori_loop(0, nblk, finish, 0)


def _hgrn(p, lb, gn, s0):
    B, L, _ = p.shape
    H = A_HEADS
    hps = HGRN_HEADS_PER_STEP
    w = hps * LANES

    def col(off):
        return pl.BlockSpec((1, L, w), lambda b, h, off=off: (b, 0, off + h))

    st_spec = pl.BlockSpec((1, 2, hps, A_DK, A_DV), lambda b, h: (b, 0, h, 0, 0))
    ng = H // hps
    return pl.pallas_call(
        functools.partial(_hgrn_kernel, nblk=L // LANES),
        grid=(B, ng),
        in_specs=[col(0), col(ng), col(2 * ng), col(3 * ng), col(4 * ng),
                  pl.BlockSpec((hps, 1, A_DK), lambda b, h: (h, 0, 0)),
                  pl.BlockSpec((1, A_DV), lambda b, h: (0, 0)),
                  st_spec],
        out_specs=[pl.BlockSpec((1, L, w), lambda b, h: (b, 0, h)), st_spec],
        out_shape=[jax.ShapeDtypeStruct((B, L, H * A_DV), BF16), jax.ShapeDtypeStruct((B, 2, H, A_DK, A_DV), F32)],
        scratch_shapes=[pltpu.VMEM((L, w), F32), pltpu.VMEM((2, L, w), BF16),
                        pltpu.VMEM((2, L // LANES, A_DV, hps * HGRN_NPAIR * A_DK), F32),
                        pltpu.VMEM((2, L // LANES, 8, w), F32)],
        compiler_params=_cparams(2),
        name="hgrn",
    )(p, p, p, p, p, lb.reshape(H, 1, A_DK), gn.reshape(1, A_DV), s0)


def _rope_tables(L, width):
    quarter = C_DH // 4
    t = jnp.arange(L)
    rows = (t // GRID_W).astype(F32)
    cols = (t % GRID_W).astype(F32)
    inv = ROPE_BASE ** (-jnp.arange(quarter, dtype=F32) / quarter)
    ang_r = rows[:, None] * inv[None, :]
    ang_c = cols[:, None] * inv[None, :]
    cos = jnp.concatenate([jnp.cos(ang_r), jnp.cos(ang_r), jnp.cos(ang_c), jnp.cos(ang_c)], -1)
    sin = jnp.concatenate([-jnp.sin(ang_r), jnp.sin(ang_r), -jnp.sin(ang_c), jnp.sin(ang_c)], -1)
    reps = width // C_DH
    return jnp.tile(cos, (1, reps)), jnp.tile(sin, (1, reps))


def _rope(x, cos, sin):
    n = x.shape[-1]
    q = C_DH // 4
    lane = lax.broadcasted_iota(I32, x.shape, x.ndim - 1)
    partner = jnp.where((lane & q) == 0, pltpu.roll(x, n - q, axis=x.ndim - 1), pltpu.roll(x, q, axis=x.ndim - 1))
    return x * cos + partner * sin


def _diff_kernel(*refs, L, n_ctx, lam_init, tq):
    if n_ctx:
        q_ref, k_ref, v_ref, lam_ref, gn_ref, kc_ref, vc_ref, cos_ref, sin_ref, o_ref, kk_scr, vt_scr, s_scr = refs
    else:
        q_ref, k_ref, v_ref, lam_ref, gn_ref, o_ref, kk_scr, vt_scr, s_scr = refs
    lam = lam_ref[...]
    lam_full = (jnp.exp(jnp.sum(lam[0:1] * lam[1:2], keepdims=True)) - jnp.exp(jnp.sum(lam[2:3] * lam[3:4], keepdims=True))
                + lam_init)
    gn = gn_ref[...]
    lane_k = lax.broadcasted_iota(I32, (L, LANES), 1)
    k_lat = k_ref[0]
    if n_ctx:
        k_lat = _rope(k_lat, cos_ref[...], sin_ref[...])
        lane_c = lax.broadcasted_iota(I32, (n_ctx, LANES), 1)
        k_ctx = kc_ref[0, 0, 0]
        for i in (0, 1):
            kk_scr[i, 0:n_ctx, :] = jnp.where((lane_c >= B_DH) if i else (lane_c < B_DH), k_ctx, 0.0).astype(BF16)
        vt_scr[:, 0:n_ctx] = vc_ref[0, 0, 0].T.astype(BF16)
    for i in (0, 1):
        kk_scr[i, n_ctx:, :] = jnp.where((lane_k >= B_DH) if i else (lane_k < B_DH), k_lat, 0.0).astype(BF16)
    vt_scr[:, n_ctx:] = v_ref[0].T.astype(BF16)
    qscale = B_DH ** -0.5 * LOG2E

    def scores(j, slot):
        rows = slice(j * tq, (j + 1) * tq)
        q = q_ref[0, rows, :]
        if n_ctx:
            q = _rope(q, cos_ref[rows, :], sin_ref[rows, :])
        qb = (q * qscale).astype(BF16)
        for i in (0, 1):
            s_scr[slot, i] = lax.dot_general(kk_scr[i], qb, (((1,), (1,)), ((), ())), preferred_element_type=F32)

    def attend(j, slot):
        vt = vt_scr[...]
        outs = []
        for i in (0, 1):
            s = s_scr[slot, i]
            m = jnp.max(s, axis=0, keepdims=True)
            e = jnp.exp2(s - m)
            den = jnp.sum(e, axis=0, keepdims=True)
            outs.append(jnp.dot(vt, e.astype(BF16), preferred_element_type=F32) / den)
        o = outs[0] - lam_full * outs[1]
        ms = jnp.mean(jnp.square(o), axis=0, keepdims=True)
        o_ref[0, j * tq:(j + 1) * tq, :] = ((o * lax.rsqrt(ms + LN_EPS)).T * gn * (1.0 - lam_init)).astype(BF16)

    nq = L // tq
    scores(0, 0)
    for j in range(nq):
        if j + 1 < nq:
            scores(j + 1, (j + 1) % 2)
        attend(j, j % 2)


def _diff_attn(p, lam, gn, lam_init, ctx=None):
    B, L, _ = p.shape
    H = B_HEADS
    off = 5 * A_HEADS

    def col(o):
        return pl.BlockSpec((1, L, LANES), lambda b, h, o=o: (b, 0, o + h))

    in_specs = [col(off), col(off + H), col(off + 2 * H),
                pl.BlockSpec(lam.shape, lambda b, h: (0, 0)),
                pl.BlockSpec((1, LANES), lambda b, h: (0, 0))]
    args = [p, p, p, lam, gn.reshape(1, LANES)]
    n_ctx = 0
    if ctx is not None:
        kc, vc, j = ctx
        n_ctx = kc.shape[3]
        cos, sin = _rope_tables(L, LANES)
        in_specs += [pl.BlockSpec((1, 1, 1, n_ctx, LANES), lambda b, h, j=j: (b, j, h, 0, 0)),
                     pl.BlockSpec((1, 1, 1, n_ctx, LANES), lambda b, h, j=j: (b, j, h, 0, 0)),
                     pl.BlockSpec((L, LANES), lambda b, h: (0, 0)),
                     pl.BlockSpec((L, LANES), lambda b, h: (0, 0))]
        args += [kc, vc, cos, sin]
    tq = min(L, ATTN_QUERIES)
    return pl.pallas_call(
        functools.partial(_diff_kernel, L=L, n_ctx=n_ctx, lam_init=lam_init, tq=tq),
        grid=(B, H),
        in_specs=in_specs,
        out_specs=pl.BlockSpec((1, L, LANES), lambda b, h: (b, 0, h)),
        out_shape=jax.ShapeDtypeStruct((B, L, H * LANES), BF16),
        scratch_shapes=[pltpu.VMEM((2, n_ctx + L, LANES), BF16), pltpu.VMEM((LANES, n_ctx + L), BF16),
                        pltpu.VMEM((2, 2, n_ctx + L, tq), F32)],
        compiler_params=_cparams(2),
        name="diff_attn",
    )(*args)


def _gqa_place_keys(k2, kp_scr):
    lane = lax.broadcasted_iota(I32, k2.shape, 1)
    swapped = pltpu.roll(k2, C_DH, axis=1)
    for kh in (0, 1):
        for u in (0, 1):
            src = k2 if kh == u else swapped
            kp_scr[kh * 2 + u] = jnp.where((lane >= C_DH) if u else (lane < C_DH), src, 0.0).astype(BF16)


def _sink_row(sink_ref, h0, h1, n):
    lane = lax.broadcasted_iota(I32, (1, 2 * n), 1)
    return jnp.where(lane < n, sink_ref[h0], sink_ref[h1]) * LOG2E


def _q_pair_rows(qb, kh):
    c0 = kh * C_GROUP * C_DH
    return jnp.concatenate([qb[:, c0:c0 + LANES], qb[:, c0 + LANES:c0 + 2 * LANES]], axis=0)


def _gqa_full_kernel(sink_ref, q_ref, k_ref, v_ref, o_ref, kp_scr, *, L):
    hp = pl.program_id(1)
    _gqa_place_keys(k_ref[0], kp_scr)
    vt = v_ref[0].T.astype(BF16)
    qb = (q_ref[0] * (C_DH ** -0.5 * LOG2E)).astype(BF16)
    pieces = []
    for kh in (0, 1):
        h0 = (hp * 2 + kh) * C_GROUP
        rhs = _q_pair_rows(qb, kh)
        outs = [None] * C_GROUP
        for u in (0, 1):
            s = lax.dot_general(kp_scr[kh * 2 + u], rhs, (((1,), (1,)), ((), ())), preferred_element_type=F32)
            sink = _sink_row(sink_ref, h0 + u, h0 + 2 + u, L)
            m = jnp.maximum(jnp.max(s, axis=0, keepdims=True), sink)
            e = jnp.exp2(s - m)
            den = jnp.sum(e, axis=0, keepdims=True) + jnp.exp2(sink - m)
            ot = jnp.dot(vt[kh * C_DH:(kh + 1) * C_DH], e.astype(BF16), preferred_element_type=F32) / den
            outs[u], outs[2 + u] = ot[:, 0:L], ot[:, L:2 * L]
        pieces += outs
    o_ref[0] = jnp.concatenate(pieces, axis=0).T.astype(BF16)


def _gqa_win_kernel(sink_ref, q_ref, k_ref, v_ref, kc_ref, vc_ref, cosq_ref, sinq_ref, o_ref,
                    kp_scr, kcp_scr, vt_scr, vct_scr, sl_scr, sc_scr, *, L):
    hp = pl.program_id(1)
    tq = WINDOW
    nblk = L // tq
    qscale = C_DH ** -0.5 * LOG2E
    _gqa_place_keys(_rope(k_ref[0], cosq_ref[:, 0:LANES], sinq_ref[:, 0:LANES]), kp_scr)
    for kh in (0, 1):
        kc = kc_ref[0, 0, kh]
        zero = jnp.zeros_like(kc)
        kcp_scr[kh * 2] = jnp.concatenate([kc, zero], axis=1).astype(BF16)
        kcp_scr[kh * 2 + 1] = jnp.concatenate([zero, kc], axis=1).astype(BF16)
        vct_scr[kh] = vc_ref[0, 0, kh].T.astype(BF16)
    for j in range(nblk):
        vt_scr[j] = v_ref[0, j * tq:(j + 1) * tq, :].T.astype(BF16)

    k_in = lax.broadcasted_iota(I32, (tq, 2 * tq), 0)
    q_in = lax.broadcasted_iota(I32, (tq, 2 * tq), 1) % tq
    keep_prev = k_in >= q_in
    keep_next = k_in <= q_in

    def window(n):
        return [b for b in (n - 1, n, n + 1) if 0 <= b < nblk]

    def scores(n, slot):
        rows = slice(n * tq, (n + 1) * tq)
        qb = (_rope(q_ref[0, rows, :], cosq_ref[rows, :], sinq_ref[rows, :]) * qscale).astype(BF16)
        blocks = window(n)
        k0, nk = blocks[0] * tq, len(blocks) * tq
        for kh in (0, 1):
            rhs = _q_pair_rows(qb, kh)
            for u in (0, 1):
                sl_scr[slot, kh * 2 + u, 0:nk, :] = lax.dot_general(kp_scr[kh * 2 + u, k0:k0 + nk, :], rhs,
                                                                    (((1,), (1,)), ((), ())), preferred_element_type=F32)
                sc_scr[slot, kh * 2 + u] = lax.dot_general(kcp_scr[kh * 2 + u], rhs, (((1,), (1,)), ((), ())),
                                                           preferred_element_type=F32)

    def attend(n, slot):
        blocks = window(n)
        pieces = []
        for kh in (0, 1):
            h0 = (hp * 2 + kh) * C_GROUP
            vt_loc = jnp.concatenate([vt_scr[b, kh * C_DH:(kh + 1) * C_DH, :] for b in blocks], axis=1)
            outs = [None] * C_GROUP
            for u in (0, 1):
                parts = []
                for t, b in enumerate(blocks):
                    s = sl_scr[slot, kh * 2 + u, t * tq:(t + 1) * tq, :]
                    if b != n:
                        s = jnp.where(keep_prev if b < n else keep_next, s, NEG_INF)
                    parts.append(s)
                s_loc = jnp.concatenate(parts, axis=0)
                s_ctx = sc_scr[slot, kh * 2 + u]
                sink = _sink_row(sink_ref, h0 + u, h0 + 2 + u, tq)
                m = jnp.maximum(jnp.maximum(jnp.max(s_loc, axis=0, keepdims=True), jnp.max(s_ctx, axis=0, keepdims=True)), sink)
                e_loc = jnp.exp2(s_loc - m)
                e_ctx = jnp.exp2(s_ctx - m)
                den = jnp.sum(e_loc, axis=0, keepdims=True) + jnp.sum(e_ctx, axis=0, keepdims=True) + jnp.exp2(sink - m)
                ot = (jnp.dot(vt_loc, e_loc.astype(BF16), preferred_element_type=F32)
                      + jnp.dot(vct_scr[kh], e_ctx.astype(BF16), preferred_element_type=F32)) / den
                outs[u], outs[2 + u] = ot[:, 0:tq], ot[:, tq:2 * tq]
            pieces += outs
        o_ref[0, n * tq:(n + 1) * tq, :] = jnp.concatenate(pieces, axis=0).T.astype(BF16)

    scores(0, 0)
    for n in range(nblk):
        if n + 1 < nblk:
            scores(n + 1, (n + 1) % 2)
        attend(n, n % 2)


def _gqa(p, sink, ctx=None):
    B, L, _ = p.shape
    qw = 2 * C_GROUP * C_DH
    nq = C_HEADS * C_DH // LANES
    nk = C_KV_HEADS * C_DH // LANES
    in_specs = [pl.BlockSpec(memory_space=pltpu.SMEM),
                pl.BlockSpec((1, L, qw), lambda b, h: (b, 0, h)),
                pl.BlockSpec((1, L, LANES), lambda b, h: (b, 0, nq + h)),
                pl.BlockSpec((1, L, LANES), lambda b, h: (b, 0, nq + nk + h))]
    args = [sink, p, p, p]
    scratch = [pltpu.VMEM((4, L, LANES), BF16)]
    if ctx is None:
        body = functools.partial(_gqa_full_kernel, L=L)
    else:
        kc, vc, j = ctx
        n_ctx = kc.shape[3]
        cos, sin = _rope_tables(L, qw)
        in_specs += [pl.BlockSpec((1, 1, 2, n_ctx, C_DH), lambda b, h, j=j: (b, j, h, 0, 0)),
                     pl.BlockSpec((1, 1, 2, n_ctx, C_DH), lambda b, h, j=j: (b, j, h, 0, 0)),
                     pl.BlockSpec((L, qw), lambda b, h: (0, 0)),
                     pl.BlockSpec((L, qw), lambda b, h: (0, 0))]
        args += [kc, vc, cos, sin]
        scratch += [pltpu.VMEM((4, n_ctx, LANES), BF16), pltpu.VMEM((L // WINDOW, LANES, WINDOW), BF16),
                    pltpu.VMEM((2, C_DH, n_ctx), BF16),
                    pltpu.VMEM((2, 4, 3 * WINDOW, 2 * WINDOW), F32), pltpu.VMEM((2, 4, n_ctx, 2 * WINDOW), F32)]
        body = functools.partial(_gqa_win_kernel, L=L)
    return pl.pallas_call(
        body,
        grid=(B, C_KV_HEADS // 2),
        in_specs=in_specs,
        out_specs=pl.BlockSpec((1, L, qw), lambda b, h: (b, 0, h)),
        out_shape=jax.ShapeDtypeStruct((B, L, C_HEADS * C_DH), BF16),
        scratch_shapes=scratch,
        compiler_params=_cparams(2),
        name="gqa",
    )(*args)


def _layer_norm(t, g, b):
    mu = jnp.mean(t, axis=-1, keepdims=True)
    tc = t - mu
    var = jnp.mean(jnp.square(tc), axis=-1, keepdims=True)
    return tc * lax.rsqrt(var + LN_EPS) * g + b


def _outproj_kernel(*refs, n_in):
    y_ref = refs[0]
    a_refs = refs[1:1 + n_in]
    mod_ref, w_ref, lng_ref, lnb_ref, wr_ref, yo_ref, h2_ref, lg_ref = refs[1 + n_in:]
    mix = None
    k0 = 0
    for a_ref in a_refs:
        kw = a_ref.shape[2]
        part = jnp.dot(a_ref[0].astype(BF16), w_ref[k0:k0 + kw, :], preferred_element_type=F32)
        mix = part if mix is None else mix + part
        k0 += kw
    yn = _layer_norm(ALPHA * y_ref[0] + mod_ref[0, 2:3, :] * mix, lng_ref[...], lnb_ref[...])
    yo_ref[0] = yn
    h2 = yn * (1.0 + mod_ref[0, 4:5, :]) + mod_ref[0, 3:4, :]
    h_hi = h2.astype(BF16)
    h2_ref[0] = h_hi
    h_lo = (h2 - h_hi.astype(F32)).astype(BF16)
    both = jnp.dot(h_hi, wr_ref[...], preferred_element_type=F32)
    lg_ref[0] = (both[:, 0:LANES] + both[:, LANES:2 * LANES]
                 + jnp.dot(h_lo, wr_ref[:, 0:LANES], preferred_element_type=F32))


def _outproj(y, mixes, mod, w_bf16, lng, lnb, wr_pad, per_batch):
    B, L, D = y.shape
    tm = min(L, PROJ_ROWS)
    n_in = len(mixes)
    in_specs = [pl.BlockSpec((1, tm, D), lambda b, i: (b, i, 0))]
    in_specs += [pl.BlockSpec((1, tm, m.shape[2]), lambda b, i: (b, i, 0)) for m in mixes]
    in_specs += [_mod_spec(per_batch, 2),
                 pl.BlockSpec(w_bf16.shape, lambda b, i: (0, 0)),
                 pl.BlockSpec((1, D), lambda b, i: (0, 0)),
                 pl.BlockSpec((1, D), lambda b, i: (0, 0)),
                 pl.BlockSpec((D, 2 * LANES), lambda b, i: (0, 0))]
    return pl.pallas_call(
        functools.partial(_outproj_kernel, n_in=n_in),
        grid=(B, L // tm),
        in_specs=in_specs,
        out_specs=[pl.BlockSpec((1, tm, D), lambda b, i: (b, i, 0)),
                   pl.BlockSpec((1, tm, D), lambda b, i: (b, i, 0)),
                   pl.BlockSpec((1, tm, LANES), lambda b, i: (b, i, 0))],
        out_shape=[jax.ShapeDtypeStruct((B, L, D), F32), jax.ShapeDtypeStruct((B, L, D), BF16),
                   jax.ShapeDtypeStruct((B, L, LANES), F32)],
        compiler_params=_cparams(2),
        name="outproj",
    )(y, *mixes, mod, w_bf16, lng.reshape(1, D), lnb.reshape(1, D), wr_pad)


def _route_kernel(lg_ref, upper_ref, pos_ref, gs_ref, aff_scr, pos_scr, *, B, N, cap):
    E = N_EXPERTS

    def affinities(b, carry):
        lg = lg_ref[b]
        lane = lax.broadcasted_iota(I32, lg.shape, 1)
        lgm = jnp.where(lane < E, lg, -jnp.inf)
        m = jnp.max(lgm, axis=-1, keepdims=True)
        ex = jnp.exp(lgm - m)
        aff = ex / jnp.sum(ex, axis=-1, keepdims=True)
        aff_scr[pl.ds(pl.multiple_of(b * E, E), E), :] = aff.T[0:E, :]
        return carry

    lax.fori_loop(0, B, affinities, 0)
    aff_all = aff_scr[...]

    def bisect(_, carry):
        lo, hi = carry
        mid = lo + jnp.right_shift(hi - lo + 1, 1)
        cnt = jnp.sum((aff_all >= pltpu.bitcast(mid, F32)).astype(I32), axis=-1, keepdims=True)
        ok = cnt >= cap
        return jnp.where(ok, mid, lo), jnp.where(ok, hi, mid - 1)

    lo0 = jnp.zeros((B * E, 1), I32)
    hi0 = jnp.full((B * E, 1), 0x7F800000, I32)
    thr, _ = lax.fori_loop(0, 31, bisect, (lo0, hi0))
    thr_f = pltpu.bitcast(thr, F32)
    gt = aff_all > thr_f
    eq = aff_all == thr_f
    need = cap - jnp.sum(gt.astype(I32), axis=-1, keepdims=True)
    upper = upper_ref[...]
    tie_rank = jnp.dot(jnp.where(eq, 1.0, 0.0).astype(BF16), upper, preferred_element_type=F32)
    sel = jnp.logical_or(gt, jnp.logical_and(eq, tie_rank < need.astype(F32)))
    slot = jnp.dot(jnp.where(sel, 1.0, 0.0).astype(BF16), upper, preferred_element_type=F32)
    pos_scr[...] = jnp.where(sel, slot.astype(I32), -1)

    def select(b, carry):
        rows = pl.ds(pl.multiple_of(b * E, E), E)
        aff_t = aff_scr[rows, :]
        pos = pos_scr[rows, :]
        pos_ref[b] = pos
        slot_id = lax.broadcasted_iota(I32, (cap, N), 0)
        for e in range(E):
            hit = slot_id == pos[e:e + 1, :]
            gcol = jnp.sum(jnp.where(hit, aff_t[e:e + 1, :], 0.0), axis=-1, keepdims=True)
            gs_ref[b, e] = jnp.broadcast_to(gcol, (cap, LANES))
        return carry

    lax.fori_loop(0, B, select, 0)


def _route(logits):
    B, N, _ = logits.shape
    E = N_EXPERTS
    cap = (CAP_FACTOR * N) // E
    upper = (jnp.arange(N)[:, None] < jnp.arange(N)[None, :]).astype(BF16)
    return pl.pallas_call(
        functools.partial(_route_kernel, B=B, N=N, cap=cap),
        grid=(1,),
        in_specs=[pl.BlockSpec((B, N, LANES), lambda i: (0, 0, 0)),
                  pl.BlockSpec((N, N), lambda i: (0, 0))],
        out_specs=[pl.BlockSpec((B, E, N), lambda i: (0, 0, 0)),
                   pl.BlockSpec((B, E, cap, LANES), lambda i: (0, 0, 0, 0))],
        out_shape=[jax.ShapeDtypeStruct((B, E, N), I32),
                   jax.ShapeDtypeStruct((B, E, cap, LANES), F32)],
        scratch_shapes=[pltpu.VMEM((B * E, N), F32), pltpu.VMEM((B * E, N), I32)],
        compiler_params=_cparams(1),
        name="route",
    )(logits, upper)


def _gather_kernel(pos_ref, h_ref, xs_ref, *, N, cap):
    slot_id = lax.broadcasted_iota(I32, (cap, N), 0)
    onehot = jnp.concatenate(
        [jnp.where(slot_id == pos_ref[0, e:e + 1, :], 1.0, 0.0).astype(BF16) for e in range(N_EXPERTS)], axis=0)
    xs = jnp.dot(onehot, h_ref[0], preferred_element_type=F32).astype(BF16)
    xs_ref[...] = xs.reshape(N_EXPERTS, cap, xs.shape[1])


def _gather(pos, h2):
    B, N, D = h2.shape
    cap = (CAP_FACTOR * N) // N_EXPERTS
    return pl.pallas_call(
        functools.partial(_gather_kernel, N=N, cap=cap),
        grid=(B,),
        in_specs=[pl.BlockSpec((1, N_EXPERTS, N), lambda b: (b, 0, 0)),
                  pl.BlockSpec((1, N, D), lambda b: (b, 0, 0))],
        out_specs=pl.BlockSpec((N_EXPERTS, cap, D), lambda b: (0, b, 0)),
        out_shape=jax.ShapeDtypeStruct((N_EXPERTS, B * cap, D), BF16),
        compiler_params=_cparams(1),
        name="gather",
    )(pos, h2)


def _ffn_kernel(xp_ref, xs_ref, gp_ref, gs_ref, wg_ref, wu_ref, wd_ref, op_ref, os_ref, *, tr):
    wg = wg_ref[0, 0].astype(BF16)
    wu = wu_ref[0, 0].astype(BF16)
    wd = wd_ref[0, 0].astype(BF16)
    for x_ref, g_ref, o_ref in ((xp_ref, gp_ref, op_ref), (xs_ref, gs_ref, os_ref)):
        rows = x_ref.shape[1]
        cap = g_ref.shape[2]
        for r0 in range(0, rows, tr):
            x = x_ref[0, r0:r0 + tr, :]
            hid = jax.nn.silu(jnp.dot(x, wg, preferred_element_type=F32)) * jnp.dot(x, wu, preferred_element_type=F32)
            out = jnp.dot(hid.astype(BF16), wd, preferred_element_type=F32)
            gate = g_ref[r0 // cap:(r0 + tr) // cap, 0].reshape(tr, LANES)[:, 0:1]
            o_ref[0, r0:r0 + tr, :] = (out * gate).astype(BF16)


def _ffn(xs_p, xs_s, gs_p, gs_s, wg, wu, wd, l):
    E, rp, D = xs_p.shape
    rs = xs_s.shape[1]
    FF = wg.shape[3]
    tr = FFN_ROWS

    def gspec(g):
        return pl.BlockSpec((g.shape[0], 1, g.shape[2], LANES), lambda e: (0, e, 0, 0))

    return pl.pallas_call(
        functools.partial(_ffn_kernel, tr=tr),
        grid=(E,),
        in_specs=[pl.BlockSpec((1, rp, D), lambda e: (e, 0, 0)),
                  pl.BlockSpec((1, rs, D), lambda e: (e, 0, 0)),
                  gspec(gs_p), gspec(gs_s),
                  pl.BlockSpec((1, 1, D, FF), lambda e, l=l: (l, e, 0, 0)),
                  pl.BlockSpec((1, 1, D, FF), lambda e, l=l: (l, e, 0, 0)),
                  pl.BlockSpec((1, 1, FF, D), lambda e, l=l: (l, e, 0, 0))],
        out_specs=[pl.BlockSpec((1, rp, D), lambda e: (e, 0, 0)),
                   pl.BlockSpec((1, rs, D), lambda e: (e, 0, 0))],
        out_shape=[jax.ShapeDtypeStruct((E, rp, D), BF16), jax.ShapeDtypeStruct((E, rs, D), BF16)],
        compiler_params=_cparams(1),
        name="ffn",
    )(xs_p, xs_s, gs_p, gs_s, wg, wu, wd)


def _combine_kernel(pos_ref, out_ref, y_ref, mod_ref, lng_ref, lnb_ref, yo_ref, *, N, cap):
    slot_id = lax.broadcasted_iota(I32, (cap, N), 0)
    onehot = jnp.concatenate(
        [jnp.where(slot_id == pos_ref[0, e:e + 1, :], 1.0, 0.0).astype(BF16) for e in range(N_EXPERTS)], axis=0)
    outs = out_ref[...].reshape(N_EXPERTS * cap, out_ref.shape[2])
    ff = lax.dot_general(onehot, outs, (((0,), (0,)), ((), ())), preferred_element_type=F32)
    yo_ref[0] = _layer_norm(ALPHA * y_ref[0] + mod_ref[0, 5:6, :] * ff, lng_ref[...], lnb_ref[...])


def _combine(pos, outs, y, mod, lng, lnb, per_batch):
    B, N, D = y.shape
    cap = (CAP_FACTOR * N) // N_EXPERTS
    return pl.pallas_call(
        functools.partial(_combine_kernel, N=N, cap=cap),
        grid=(B,),
        in_specs=[pl.BlockSpec((1, N_EXPERTS, N), lambda b: (b, 0, 0)),
                  pl.BlockSpec((N_EXPERTS, cap, D), lambda b: (0, b, 0)),
                  pl.BlockSpec((1, N, D), lambda b: (b, 0, 0)),
                  _mod_spec(per_batch, 1),
                  pl.BlockSpec((1, D), lambda b: (0, 0)),
                  pl.BlockSpec((1, D), lambda b: (0, 0))],
        out_specs=pl.BlockSpec((1, N, D), lambda b: (b, 0, 0)),
        out_shape=jax.ShapeDtypeStruct((B, N, D), F32),
        compiler_params=_cparams(1),
        name="combine",
    )(pos, outs, y, mod, lng.reshape(1, D), lnb.reshape(1, D))


def kernel(x_prompt, x_sample, state_hgrn, cache_diff_k, cache_diff_v, cache_win_k, cache_win_v, c, c_ctx, w_mod, b_mod, ln_g, ln_b, w_in_even, w_out_even, hgrn_lb_logits, hgrn_norm_g, diff_lambda, diff_norm_g, w_in_odd, w_out_odd, win_sink, w_router, w_exp_gate, w_exp_up, w_exp_down):
    yp, ys = x_prompt, x_sample
    nb_s = ys.shape[0]
    lb_all = jnp.cumsum(jax.nn.softmax(hgrn_lb_logits.astype(F32), axis=0), axis=0)
    cvec = jnp.zeros((16, D_MODEL), F32).at[:nb_s].set(c).at[nb_s].set(c_ctx)
    mods = _modulation(cvec, w_mod, b_mod).reshape(DEPTH, 16, 6, D_MODEL)
    mods = jnp.pad(mods, ((0, 0), (0, 0), (0, 2), (0, 0)))
    wr_f32 = jnp.pad(w_router, ((0, 0), (0, 0), (0, LANES - N_EXPERTS)))
    wr_hi = wr_f32.astype(BF16)
    wr_pad = jnp.concatenate([wr_hi, (wr_f32 - wr_hi.astype(F32)).astype(BF16)], axis=-1)
    new_state = new_dk = new_dv = new_wk = new_wv = None
    for l in range(DEPTH):
        mod_p = mods[l, nb_s:nb_s + 1]
        mod_s = mods[l, :nb_s]
        j = l // 2
        if l % 2 == 0:
            lam_init = 0.8 - 0.6 * math.exp(-0.3 * l)
            w_in = w_in_even[j].astype(BF16)
            w_out = w_out_even[j].astype(BF16)
            k0 = 3 * A_HEADS * A_DK + 2 * A_HEADS * A_DV + B_HEADS * 2 * B_DH
            pp, k_new, v_new = _inproj(yp, mod_p, w_in, False, kv=(k0, B_HEADS, 2 * B_DH))
            ps, = _inproj(ys, mod_s, w_in, True)
            new_dk, new_dv = k_new[:, None], v_new[:, None]
            zero_state = jnp.zeros((yp.shape[0], 2, A_HEADS, A_DK, A_DV), F32)
            oa_p, st_p = _hgrn(pp, lb_all[l], hgrn_norm_g[j], zero_state)
            oa_s, _ = _hgrn(ps, lb_all[l], hgrn_norm_g[j], state_hgrn[:, j])
            ob_p = _diff_attn(pp, diff_lambda[j], diff_norm_g[j], lam_init)
            ob_s = _diff_attn(ps, diff_lambda[j], diff_norm_g[j], lam_init, ctx=(cache_diff_k, cache_diff_v, j))
            mix_p, mix_s = [oa_p, ob_p], [oa_s, ob_s]
            new_state = st_p[:, None]
        else:
            w_in = w_in_odd[j].astype(BF16)
            w_out = w_out_odd[j].astype(BF16)
            pp, k_new, v_new = _inproj(yp, mod_p, w_in, False, kv=(C_HEADS * C_DH, C_KV_HEADS, C_DH))
            ps, = _inproj(ys, mod_s, w_in, True)
            new_wk, new_wv = k_new[:, None], v_new[:, None]
            mix_p = [_gqa(pp, win_sink[j])]
            mix_s = [_gqa(ps, win_sink[j], ctx=(cache_win_k, cache_win_v, j))]
        yp, h2p, lgp = _outproj(yp, mix_p, mod_p, w_out, ln_g[l, 0], ln_b[l, 0], wr_pad[l], False)
        ys, h2s, lgs = _outproj(ys, mix_s, mod_s, w_out, ln_g[l, 0], ln_b[l, 0], wr_pad[l], True)
        pos_p, gs_p = _route(lgp)
        pos_s, gs_s = _route(lgs)
        xs_p = _gather(pos_p, h2p)
        xs_s = _gather(pos_s, h2s)
        out_p, out_s = _ffn(xs_p, xs_s, gs_p, gs_s, w_exp_gate, w_exp_up, w_exp_down, l)
        yp = _combine(pos_p, out_p, yp, mod_p, ln_g[l, 1], ln_b[l, 1], False)
        ys = _combine(pos_s, out_s, ys, mod_s, ln_g[l, 1], ln_b[l, 1], True)
    return (yp, ys, new_state, new_dk, new_dv, new_wk, new_wv)
```
